```python
import math
import jax, jax.numpy as jnp
from jax import lax
import numpy as np

D_MODEL = 1024
BATCH = 16
SEQ = 2048
DEPTH = 1

PLE_DIM = 256
D_RNN = 1280
RNN_BLOCKS = 10
RNN_BLOCK_W = D_RNN // RNN_BLOCKS
CONV_W = 4
LRU_C = 8.0
HEAD_DIM = 128
HEADS_PER_GROUP = 4
ATTN_PATTERNS = ((128, 1), (512, 4), (2048, 16))
N_GROUPS = len(ATTN_PATTERNS)
ATT_W = HEADS_PER_GROUP * HEAD_DIM
QKV_W = N_GROUPS * 3 * ATT_W
N_BRANCH = 2
N_IN = 2 * D_RNN + QKV_W + ATT_W + N_BRANCH * D_MODEL
ROPE_THETA = 10000.0
EPS = 1e-6

OFF_Z_RNN = D_RNN
OFF_QKV = 2 * D_RNN
OFF_Z_ATT = OFF_QKV + QKV_W
OFF_GATES = OFF_Z_ATT + ATT_W

kernel_name = "hybrid_rglru_dilated_attn_block"


def rms_norm(x, gain):
    xf = x.astype(jnp.float32)
    var = jnp.mean(xf * xf, axis=-1, keepdims=True)
    return (xf * lax.rsqrt(var + EPS) * gain.astype(jnp.float32)).astype(x.dtype)


def rope(t, cos, sin):
    t1, t2 = jnp.split(t.astype(jnp.float32), 2, axis=-1)
    c = cos[None, :, None, None, :]
    s = sin[None, :, None, None, :]
    return jnp.concatenate([t1 * c - t2 * s, t2 * c + t1 * s], axis=-1).astype(t.dtype)


def causal_depthwise_conv(x, w, b):
    s = x.shape[1]
    xp = jnp.pad(x, ((0, 0), (CONV_W - 1, 0), (0, 0)))
    y = b[None, None, :]
    for k in range(CONV_W):
        y = y + w[k][None, None, :] * xp[:, k:k + s]
    return y


def rg_lru(x, w_a, b_a, w_x, b_x, lam):
    b, s, _ = x.shape
    xf = x.astype(jnp.float32)
    xb = xf.reshape(b, s, RNN_BLOCKS, RNN_BLOCK_W)
    r = jax.nn.sigmoid(jnp.einsum('bsni,nij->bsnj', xb, w_a.astype(jnp.float32)).reshape(b, s, D_RNN) + b_a.astype(jnp.float32))
    i = jax.nn.sigmoid(jnp.einsum('bsni,nij->bsnj', xb, w_x.astype(jnp.float32)).reshape(b, s, D_RNN) + b_x.astype(jnp.float32))
    log_a = -LRU_C * r * jax.nn.softplus(-lam.astype(jnp.float32))[None, None, :]
    a = jnp.exp(log_a)
    mult = jnp.sqrt(-jnp.expm1(2.0 * log_a))
    mult = mult.at[:, 0].set(1.0)
    u = mult * (i * xf)

    def combine(left, right):
        a_l, u_l = left
        a_r, u_r = right
        return a_l * a_r, a_r * u_l + u_r

    _, h = lax.associative_scan(combine, (a, u), axis=1)
    return h.astype(x.dtype)


def dilated_window_attention(q, k, v, window, dilation):
    b, s, h, hd = q.shape
    blk = window // dilation
    sub_len = s // dilation
    n_blk = -(-sub_len // blk)
    padded = n_blk * blk

    def sub(t):
        return t.reshape(b, sub_len, dilation, h, hd)

    qb = jnp.pad(sub(q), ((0, 0), (0, padded - sub_len), (0, 0), (0, 0), (0, 0)))
    qb = qb.reshape(b, n_blk, blk, dilation, h, hd)

    def key_blocks(t):
        tp = jnp.pad(sub(t), ((0, 0), (blk, padded - sub_len), (0, 0), (0, 0), (0, 0)))
        prev = tp[:, :padded].reshape(b, n_blk, blk, dilation, h, hd)
        cur = tp[:, blk:].reshape(b, n_blk, blk, dilation, h, hd)
        return jnp.concatenate([prev, cur], axis=2)

    kb = key_blocks(k)
    vb = key_blocks(v)
    scores = jnp.einsum('bnqchd,bnkchd->bnchqk', qb, kb,
                        preferred_element_type=jnp.float32) * (hd ** -0.5)
    qi = jnp.arange(blk)[:, None]
    kj = jnp.arange(2 * blk)[None, :]
    dist = qi + blk - kj
    key_pos = jnp.arange(n_blk)[:, None, None] * blk + kj[None] - blk
    valid = ((dist >= 0) & (dist <= blk))[None] & (key_pos >= 0)
    scores = jnp.where(valid[None, :, None, None], scores, -jnp.inf)
    m = jnp.max(scores, axis=-1, keepdims=True)
    e = jnp.exp(scores - m)
    den = jnp.sum(e, axis=-1)
    o = jnp.einsum('bnchqk,bnkchd->bnqchd', e, vb.astype(jnp.float32))
    o = o / jnp.moveaxis(den, -1, 2)[..., None]
    lse = jnp.moveaxis(m[..., 0] + jnp.log(den), -1, 2)
    o = o.reshape(b, padded, dilation, h, hd)[:, :sub_len].reshape(b, s, h, hd)
    lse = lse.reshape(b, padded, dilation, h)[:, :sub_len].reshape(b, s, h)
    return o, lse


def setup_inputs(seed: int = 0) -> dict:
    key = jax.random.key(seed)
    ks = jax.random.split(key, 24)
    f32 = jnp.float32

    def nrm(k, shape, scale):
        return jax.random.normal(k, shape, f32) * scale

    u = jax.random.uniform(ks[9], (DEPTH, D_RNN), f32, minval=0.9, maxval=0.999)
    return {
        "x": nrm(ks[0], (BATCH, SEQ, D_MODEL), 1.0),
        "p": nrm(ks[1], (DEPTH, BATCH, SEQ, PLE_DIM), 1.0),
        "norm_mix": 1.0 + nrm(ks[2], (DEPTH, D_MODEL), 0.02),
        "w_in": nrm(ks[3], (DEPTH, D_MODEL, N_IN), D_MODEL ** -0.5),
        "b_in": nrm(ks[4], (DEPTH, N_IN), 0.01),
        "conv_w": nrm(ks[5], (DEPTH, CONV_W, D_RNN), CONV_W ** -0.5),
        "conv_b": nrm(ks[6], (DEPTH, D_RNN), 0.01),
        "w_rg_a": nrm(ks[7], (DEPTH, RNN_BLOCKS, RNN_BLOCK_W, RNN_BLOCK_W), RNN_BLOCK_W ** -0.5),
        "b_rg_a": nrm(ks[8], (DEPTH, D_RNN), 0.01),
        "w_rg_x": nrm(ks[10], (DEPTH, RNN_BLOCKS, RNN_BLOCK_W, RNN_BLOCK_W), RNN_BLOCK_W ** -0.5),
        "b_rg_x": nrm(ks[11], (DEPTH, D_RNN), 0.01),
        "lru_lambda": jnp.log(u) - jnp.log1p(-u),
        "q_norm": 1.0 + nrm(ks[12], (DEPTH, N_GROUPS, HEAD_DIM), 0.02),
        "k_norm": 1.0 + nrm(ks[13], (DEPTH, N_GROUPS, HEAD_DIM), 0.02),
        "w_o_rnn": nrm(ks[14], (DEPTH, D_RNN, D_MODEL), D_RNN ** -0.5),
        "w_o_att": nrm(ks[15], (DEPTH, ATT_W, D_MODEL), ATT_W ** -0.5),
        "w_out": nrm(ks[16], (DEPTH, D_MODEL, D_MODEL), D_MODEL ** -0.5),
        "norm_ple": 1.0 + nrm(ks[17], (DEPTH, D_MODEL), 0.02),
        "w_ple_gate": nrm(ks[18], (DEPTH, D_MODEL, D_MODEL), D_MODEL ** -0.5),
        "b_ple_gate": nrm(ks[19], (DEPTH, D_MODEL), 0.01),
        "w_ple": nrm(ks[20], (DEPTH, PLE_DIM, D_MODEL), PLE_DIM ** -0.5),
    }


def reference(x, p, norm_mix, w_in, b_in, conv_w, conv_b, w_rg_a, b_rg_a, w_rg_x, b_rg_x,
              lru_lambda, q_norm, k_norm, w_o_rnn, w_o_att, w_out, norm_ple, w_ple_gate,
              b_ple_gate, w_ple):
    b, s, _ = x.shape
    pos = jnp.arange(s, dtype=jnp.float32)
    inv_freq = ROPE_THETA ** (-jnp.arange(0, HEAD_DIM, 2, dtype=jnp.float32) / HEAD_DIM)
    ang = pos[:, None] * inv_freq[None, :]
    cos, sin = jnp.cos(ang), jnp.sin(ang)

    for layer in range(DEPTH):
        hn = rms_norm(x, norm_mix[layer])
        proj = jnp.einsum('bsd,dn->bsn', hn, w_in[layer]) + b_in[layer]
        x_rnn = proj[..., :OFF_Z_RNN]
        z_rnn = proj[..., OFF_Z_RNN:OFF_QKV]
        qkv = proj[..., OFF_QKV:OFF_Z_ATT]
        z_att = proj[..., OFF_Z_ATT:OFF_GATES]
        gates = jax.nn.sigmoid(proj[..., OFF_GATES:].reshape(b, s, N_BRANCH, D_MODEL))

        xc = causal_depthwise_conv(x_rnn, conv_w[layer], conv_b[layer])
        h_rnn = rg_lru(xc, w_rg_a[layer], b_rg_a[layer], w_rg_x[layer], b_rg_x[layer], lru_lambda[layer])
        y_rnn = h_rnn * jax.nn.silu(z_rnn)

        qkv = qkv.reshape(b, s, N_GROUPS, 3, HEADS_PER_GROUP, HEAD_DIM)
        q = rms_norm(qkv[:, :, :, 0], q_norm[layer][:, None, :])
        k = rms_norm(qkv[:, :, :, 1], k_norm[layer][:, None, :])
        v = qkv[:, :, :, 2]
        q = rope(q, cos, sin)
        k = rope(k, cos, sin)
        outs, lses = [], []
        for g, (window, dilation) in enumerate(ATTN_PATTERNS):
            o_g, lse_g = dilated_window_attention(q[:, :, g], k[:, :, g], v[:, :, g], window, dilation)
            outs.append(o_g)
            lses.append(lse_g)
        wts = jax.nn.softmax(jnp.stack(lses, axis=0), axis=0)
        att = jnp.sum(wts[..., None] * jnp.stack(outs, axis=0), axis=0)
        att = att.astype(x.dtype).reshape(b, s, ATT_W)
        y_att = att * jax.nn.silu(z_att)

        yr = jnp.einsum('bsc,cd->bsd', y_rnn, w_o_rnn[layer])
        ya = jnp.einsum('bsc,cd->bsd', y_att, w_o_att[layer])
        merged = gates[:, :, 0] * yr + gates[:, :, 1] * ya
        x = x + jnp.einsum('bsd,de->bse', merged, w_out[layer])

        pe = jnp.einsum('bsk,kd->bsd', p[layer], w_ple[layer])
        pg = jax.nn.sigmoid(jnp.einsum('bsd,de->bse', rms_norm(x, norm_ple[layer]), w_ple_gate[layer]) + b_ple_gate[layer])
        x = x + pg * pe
    return x
```

```python
import functools

import jax
import jax.numpy as jnp
from jax import lax
from jax.experimental import pallas as pl
from jax.experimental.pallas import tpu as pltpu

D_MODEL = 1024
PLE_DIM = 256
D_RNN = 1280
RNN_BLOCKS = 10
RNN_BLOCK_W = D_RNN // RNN_BLOCKS
CONV_W = 4
LRU_C = 8.0
HEAD_DIM = 128
HEADS = 4
ATTN_PATTERNS = ((128, 1), (512, 4), (2048, 16))
N_GROUPS = len(ATTN_PATTERNS)
ATT_W = HEADS * HEAD_DIM
GROUP_W = 3 * ATT_W
ATT_BLK = 128
ROPE_THETA = 10000.0
EPS = 1e-6

OFF_Z_RNN = D_RNN
OFF_QKV = 2 * D_RNN
OFF_Z_ATT = OFF_QKV + N_GROUPS * GROUP_W
OFF_GATES = OFF_Z_ATT + ATT_W

LANES = 128
SUBLANES = 8
VMEM_LIMIT_BYTES = 56 * 1024 * 1024

TM_IN = 256
TC_RNN = 256
TM_OUT = 512
COMBINE_ROWS = 256

MM_DTYPE = jnp.bfloat16
NEG_INF = float("-inf")


def _sigmoid(v):
    return 1.0 / (1.0 + jnp.exp(-v))


def _rms_normalize(v):
    var = jnp.mean(v * v, axis=-1, keepdims=True)
    return v * lax.rsqrt(var + EPS)


def _dot(a, b):
    return jnp.dot(a, b, preferred_element_type=jnp.float32)


def _const_spec(shape):
    zeros = (0,) * len(shape)
    return pl.BlockSpec(shape, lambda *_: zeros, pipeline_mode=pl.Buffered(1))


def _params(n_grid):
    return pltpu.CompilerParams(
        dimension_semantics=("arbitrary",) * n_grid,
        vmem_limit_bytes=VMEM_LIMIT_BYTES,
    )


def _qk_epilogue(acc, gq, gk, cos2, sin2):
    outs = []
    for part, gain in ((0, gq), (1, gk)):
        for h in range(HEADS):
            lo = part * ATT_W + h * HEAD_DIM
            t = _rms_normalize(acc[:, lo:lo + HEAD_DIM]) * gain
            outs.append(t * cos2 + pltpu.roll(t, HEAD_DIM // 2, 1) * sin2)
    outs.append(acc[:, 2 * ATT_W:])
    return jnp.concatenate(outs, axis=1)


def _in_proj_attn_kernel(x_ref, nm_ref, w0_ref, b0_ref, w1_ref, b1_ref, w2_ref, b2_ref,
                         gq_ref, gk_ref, cos0_ref, sin0_ref, cos1_ref, sin1_ref,
                         cos2_ref, sin2_ref,
                         qkv0_ref, qkv1_ref, qkv2_ref, sz_ref, gates_ref, hs_ref):
    tm = x_ref.shape[0]
    n_slabs = D_MODEL // LANES
    hn = _rms_normalize(x_ref[...]) * nm_ref[...]

    hb = hn.astype(MM_DTYPE)
    acc = _dot(hb, w0_ref[:, :GROUP_W]) + b0_ref[:, :GROUP_W]
    qkv0_ref[...] = _qk_epilogue(acc, gq_ref[0:1], gk_ref[0:1], cos0_ref[...],
                                 sin0_ref[...]).astype(qkv0_ref.dtype)
    z = _dot(hb, w0_ref[:, GROUP_W:GROUP_W + ATT_W]) + b0_ref[:, GROUP_W:GROUP_W + ATT_W]
    sz_ref[...] = (z * _sigmoid(z)).astype(sz_ref.dtype)
    g = _dot(hb, w0_ref[:, GROUP_W + ATT_W:]) + b0_ref[:, GROUP_W + ATT_W:]
    gates_ref[...] = _sigmoid(g).astype(gates_ref.dtype)

    for j in range(n_slabs):
        hs_ref[j] = hn[:, j * LANES:(j + 1) * LANES]
    for (dil, w_ref, b_ref, gi, cos_ref, sin_ref, out_ref) in (
            (ATTN_PATTERNS[1][1], w1_ref, b1_ref, 1, cos1_ref, sin1_ref, qkv1_ref),
            (ATTN_PATTERNS[2][1], w2_ref, b2_ref, 2, cos2_ref, sin2_ref, qkv2_ref)):
        per = tm // dil
        classes = []
        for c in range(dil):
            classes.append(jnp.concatenate(
                [hs_ref[j, pl.ds(c, per, stride=dil), :] for j in range(n_slabs)], axis=1))
        hp = jnp.concatenate(classes, axis=0).astype(MM_DTYPE)
        acc = _dot(hp, w_ref[...]) + b_ref[...]
        cos2 = cos_ref[...].reshape(tm, HEAD_DIM)
        sin2 = sin_ref[...].reshape(tm, HEAD_DIM)
        res = _qk_epilogue(acc, gq_ref[gi:gi + 1], gk_ref[gi:gi + 1], cos2, sin2)
        out_ref[...] = res.astype(out_ref.dtype).reshape(out_ref.shape)


def _in_proj_attn(x, nm, w0, b0, w1, b1, w2, b2, gq, gk, tabs):
    B, S, _ = x.shape
    tm = TM_IN
    d1, d2 = ATTN_PATTERNS[1][1], ATTN_PATTERNS[2][1]
    (cos0, sin0), (cos1, sin1), (cos2, sin2) = tabs
    grid = (B, S // tm)
    row = lambda w: pl.BlockSpec((None, tm, w), lambda b, i: (b, i, 0))
    in_specs = [
        row(D_MODEL),
        _const_spec(nm.shape),
        _const_spec(w0.shape), _const_spec(b0.shape),
        _const_spec(w1.shape), _const_spec(b1.shape),
        _const_spec(w2.shape), _const_spec(b2.shape),
        _const_spec(gq.shape), _const_spec(gk.shape),
        pl.BlockSpec((tm, HEAD_DIM), lambda b, i: (i, 0)),
        pl.BlockSpec((tm, HEAD_DIM), lambda b, i: (i, 0)),
        pl.BlockSpec((d1, tm // d1, HEAD_DIM), lambda b, i: (0, i, 0)),
        pl.BlockSpec((d1, tm // d1, HEAD_DIM), lambda b, i: (0, i, 0)),
        pl.BlockSpec((d2, tm // d2, HEAD_DIM), lambda b, i: (0, i, 0)),
        pl.BlockSpec((d2, tm // d2, HEAD_DIM), lambda b, i: (0, i, 0)),
    ]
    out_shape = (
        jax.ShapeDtypeStruct((B, S, GROUP_W), MM_DTYPE),
        jax.ShapeDtypeStruct((B, d1, S // d1, GROUP_W), MM_DTYPE),
        jax.ShapeDtypeStruct((B, d2, S // d2, GROUP_W), MM_DTYPE),
        jax.ShapeDtypeStruct((B, S, ATT_W), MM_DTYPE),
        jax.ShapeDtypeStruct((B, S, 2 * D_MODEL), MM_DTYPE),
    )
    out_specs = (
        row(GROUP_W),
        pl.BlockSpec((None, d1, tm // d1, GROUP_W), lambda b, i: (b, 0, i, 0)),
        pl.BlockSpec((None, d2, tm // d2, GROUP_W), lambda b, i: (b, 0, i, 0)),
        row(ATT_W),
        row(2 * D_MODEL),
    )
    return pl.pallas_call(
        _in_proj_attn_kernel,
        grid=grid,
        in_specs=in_specs,
        out_specs=out_specs,
        out_shape=out_shape,
        scratch_shapes=[pltpu.VMEM((D_MODEL // LANES, tm, LANES), jnp.float32)],
        compiler_params=_params(2),
        name="in_proj_attn",
    )(x, nm, w0, b0, w1, b1, w2, b2, gq, gk, cos0, sin0, cos1, sin1, cos2, sin2)


def _rnn_kernel(x_ref, nm_ref, wr_ref, br_ref, cw_ref, cb_ref, wg_ref, bg_ref, lam_ref,
                wo_ref, yr_ref, xs_ref, h_ref):
    tc = x_ref.shape[0]
    t_idx = pl.program_id(1)

    @pl.when(t_idx == 0)
    def _():
        xs_ref[0:SUBLANES, :] = jnp.zeros((SUBLANES, D_RNN), jnp.float32)
        h_ref[...] = jnp.zeros_like(h_ref)

    hb = (_rms_normalize(x_ref[...]) * nm_ref[...]).astype(MM_DTYPE)
    xz = _dot(hb, wr_ref[...]) + br_ref[...]
    z = xz[:, D_RNN:]
    xs_ref[SUBLANES:, :] = xz[:, :D_RNN]
    xc = cb_ref[...]
    for k in range(CONV_W):
        off = SUBLANES - (CONV_W - 1) + k
        xc = xc + cw_ref[k:k + 1, :] * xs_ref[pl.ds(off, tc), :]
    xs_ref[0:SUBLANES, :] = xs_ref[tc:tc + SUBLANES, :]

    xcb = xc.astype(MM_DTYPE)
    pre = [_dot(xcb[:, n * RNN_BLOCK_W:(n + 1) * RNN_BLOCK_W], wg_ref[n])
           for n in range(RNN_BLOCKS)]
    r = _sigmoid(jnp.concatenate([t[:, :RNN_BLOCK_W] for t in pre], axis=1) + bg_ref[0:1, :])
    gi = _sigmoid(jnp.concatenate([t[:, RNN_BLOCK_W:] for t in pre], axis=1) + bg_ref[1:2, :])
    neg_lam = -lam_ref[...]
    softplus = jnp.maximum(neg_lam, 0.0) + jnp.log(1.0 + jnp.exp(-jnp.abs(neg_lam)))
    a = jnp.exp((-LRU_C) * softplus * r)
    mult = jnp.sqrt(1.0 - a * a)
    rows = lax.broadcasted_iota(jnp.int32, (tc, 1), 0) + t_idx * tc
    mult = jnp.where(rows == 0, 1.0, mult)
    u = mult * (gi * xc)

    n_grp = tc // SUBLANES
    a3 = a.reshape(n_grp, SUBLANES, D_RNN)
    u3 = u.reshape(n_grp, SUBLANES, D_RNN)
    sub = lax.broadcasted_iota(jnp.int32, (1, SUBLANES, 1), 1)
    s = 1
    while s < SUBLANES:
        keep = sub >= s
        u_sh = jnp.where(keep, pltpu.roll(u3, s, 1), 0.0)
        a_sh = jnp.where(keep, pltpu.roll(a3, s, 1), 1.0)
        u3 = a3 * u_sh + u3
        a3 = a3 * a_sh
        s *= 2
    carry = h_ref[...]
    hs = []
    for j in range(n_grp):
        hj = u3[j] + a3[j] * carry
        hs.append(hj)
        carry = jnp.broadcast_to(hj[SUBLANES - 1:SUBLANES, :], (SUBLANES, D_RNN))
    h_ref[...] = carry
    h = jnp.concatenate(hs, axis=0)

    y = (h * (z * _sigmoid(z))).astype(MM_DTYPE)
    yr_ref[...] = _dot(y, wo_ref[...]).astype(yr_ref.dtype)


def _rnn_branch(x, nm, wr, br, cw, cb, wg, bg, lam, wo):
    B, S, _ = x.shape
    tc = TC_RNN
    return pl.pallas_call(
        _rnn_kernel,
        grid=(B, S // tc),
        in_specs=[
            pl.BlockSpec((None, tc, D_MODEL), lambda b, i: (b, i, 0)),
            _const_spec(nm.shape), _const_spec(wr.shape), _const_spec(br.shape),
            _const_spec(cw.shape), _const_spec(cb.shape), _const_spec(wg.shape),
            _const_spec(bg.shape), _const_spec(lam.shape), _const_spec(wo.shape),
        ],
        out_specs=pl.BlockSpec((None, tc, D_MODEL), lambda b, i: (b, i, 0)),
        out_shape=jax.ShapeDtypeStruct((B, S, D_MODEL), MM_DTYPE),
        scratch_shapes=[
            pltpu.VMEM((tc + SUBLANES, D_RNN), jnp.float32),
            pltpu.VMEM((SUBLANES, D_RNN), jnp.float32),
        ],
        compiler_params=_params(2),
        name="rnn_branch",
    )(x, nm, wr, br, cw, cb, wg, bg, lam, wo)


def _attn_kernel(q0, k0, v0, q1, k1, v1, q2, k2, v2, sz_ref, y_ref, o_nat, l_nat):
    S = sz_ref.shape[0]
    n_blk = S // ATT_BLK
    qi = lax.broadcasted_iota(jnp.int32, (ATT_BLK, ATT_BLK), 0)
    kj = lax.broadcasted_iota(jnp.int32, (ATT_BLK, ATT_BLK), 1)
    cur_ok = kj <= qi
    nt = (((1,), (1,)), ((), ()))

    def block(g, q_ref, k_ref, v_ref, nb, with_prev, has_prev):
        r0 = nb * ATT_BLK if isinstance(nb, int) else pl.multiple_of(nb * ATT_BLK, ATT_BLK)
        q = q_ref[pl.ds(r0, ATT_BLK), :]
        if with_prev:
            p0 = pl.multiple_of(r0 - ATT_BLK, ATT_BLK)
            kk = k_ref[pl.ds(p0, 2 * ATT_BLK), :]
            vv = v_ref[pl.ds(p0, 2 * ATT_BLK), :]
            s = lax.dot_general(q, kk, nt, preferred_element_type=jnp.float32)
            prev_ok = kj >= qi + jnp.where(has_prev, 0, ATT_BLK)
            ok = jnp.concatenate([prev_ok, cur_ok], axis=1)
        else:
            kk = k_ref[pl.ds(r0, ATT_BLK), :]
            vv = v_ref[pl.ds(r0, ATT_BLK), :]
            s = lax.dot_general(q, kk, nt, preferred_element_type=jnp.float32)
            ok = cur_ok
        s = jnp.where(ok, s, NEG_INF)
        m = jnp.max(s, axis=-1, keepdims=True)
        e = jnp.exp(s - m)
        den = jnp.sum(e, axis=-1, keepdims=True)
        o = _dot(e.astype(MM_DTYPE), vv) * (1.0 / den)
        lse = jnp.broadcast_to(m + jnp.log(den), (ATT_BLK, HEAD_DIM))
        dil = ATTN_PATTERNS[g][1]
        if dil == 1:
            idx = pl.ds(r0, ATT_BLK)
        else:
            blocks_per_class = n_blk // dil
            c = nb // blocks_per_class
            m0 = (nb % blocks_per_class) * ATT_BLK
            idx = pl.ds(m0 * dil + c, ATT_BLK, stride=dil)
        o_nat[g, idx, :] = o
        l_nat[g, idx, :] = lse

    for g, (q_ref, k_ref, v_ref) in enumerate(((q0, k0, v0), (q1, k1, v1), (q2, k2, v2))):
        dil = ATTN_PATTERNS[g][1]
        blocks_per_class = n_blk // dil
        if blocks_per_class == 1:
            def body(nb, carry, g=g, q_ref=q_ref, k_ref=k_ref, v_ref=v_ref):
                block(g, q_ref, k_ref, v_ref, nb, False, None)
                return carry
            lax.fori_loop(0, n_blk, body, 0)
        else:
            block(g, q_ref, k_ref, v_ref, 0, False, None)

            def body(nb, carry, g=g, q_ref=q_ref, k_ref=k_ref, v_ref=v_ref,
                     bpc=blocks_per_class):
                block(g, q_ref, k_ref, v_ref, nb, True, (nb % bpc) != 0)
                return carry
            lax.fori_loop(1, n_blk, body, 0)

    def merge(i, carry):
        rows = pl.ds(pl.multiple_of(i * COMBINE_ROWS, COMBINE_ROWS), COMBINE_ROWS)
        ls = [l_nat[g, rows, :] for g in range(N_GROUPS)]
        m = functools.reduce(jnp.maximum, ls)
        ws = [jnp.exp(l - m) for l in ls]
        den = functools.reduce(lambda p, q: p + q, ws)
        att = functools.reduce(
            lambda p, q: p + q, [w * o_nat[g, rows, :] for g, w in enumerate(ws)])
        att = att * (1.0 / den)
        y_ref[rows, :] = (att * sz_ref[rows, :].astype(jnp.float32)).astype(y_ref.dtype)
        return carry
    lax.fori_loop(0, S // COMBINE_ROWS, merge, 0)


def _attention(qkv0, qkv1, qkv2, sz):
    B, S, _ = qkv0.shape
    qkv1 = qkv1.reshape(B, S, GROUP_W)
    qkv2 = qkv2.reshape(B, S, GROUP_W)
    in_specs = []
    for _ in range(N_GROUPS):
        for part in range(3):
            in_specs.append(pl.BlockSpec((None, S, HEAD_DIM),
                                         lambda b, h, part=part: (b, 0, part * HEADS + h)))
    in_specs.append(pl.BlockSpec((None, S, HEAD_DIM), lambda b, h: (b, 0, h)))
    return pl.pallas_call(
        _attn_kernel,
        grid=(B, HEADS),
        in_specs=in_specs,
        out_specs=pl.BlockSpec((None, S, HEAD_DIM), lambda b, h: (b, 0, h)),
        out_shape=jax.ShapeDtypeStruct((B, S, ATT_W), MM_DTYPE),
        scratch_shapes=[
            pltpu.VMEM((N_GROUPS, S, HEAD_DIM), jnp.float32),
            pltpu.VMEM((N_GROUPS, S, HEAD_DIM), jnp.float32),
        ],
        compiler_params=_params(2),
        name="attention",
    )(qkv0, qkv0, qkv0, qkv1, qkv1, qkv1, qkv2, qkv2, qkv2, sz)


def _out_kernel(x_ref, p_ref, ya_ref, yr_ref, g_ref, woa_ref, wout_ref, np_ref, wpg_ref,
                bpg_ref, wple_ref, o_ref):
    ya = _dot(ya_ref[...], woa_ref[...])
    g0 = g_ref[:, :D_MODEL].astype(jnp.float32)
    g1 = g_ref[:, D_MODEL:].astype(jnp.float32)
    merged = g0 * yr_ref[...].astype(jnp.float32) + g1 * ya
    x2 = x_ref[...] + _dot(merged.astype(MM_DTYPE), wout_ref[...])
    pe = _dot(p_ref[...].astype(MM_DTYPE), wple_ref[...])
    n2 = (_rms_normalize(x2) * np_ref[...]).astype(MM_DTYPE)
    pg = _sigmoid(_dot(n2, wpg_ref[...]) + bpg_ref[...])
    o_ref[...] = x2 + pg * pe


def _out_proj(x, p, ya, yr, gates, woa, wout, npl, wpg, bpg, wple):
    B, S, _ = x.shape
    tm = TM_OUT
    row = lambda w: pl.BlockSpec((None, tm, w), lambda b, i: (b, i, 0))
    return pl.pallas_call(
        _out_kernel,
        grid=(B, S // tm),
        in_specs=[
            row(D_MODEL), row(PLE_DIM), row(ATT_W), row(D_MODEL), row(2 * D_MODEL),
            _const_spec(woa.shape), _const_spec(wout.shape), _const_spec(npl.shape),
            _const_spec(wpg.shape), _const_spec(bpg.shape), _const_spec(wple.shape),
        ],
        out_specs=row(D_MODEL),
        out_shape=jax.ShapeDtypeStruct((B, S, D_MODEL), x.dtype),
        compiler_params=_params(2),
        name="out_proj",
    )(x, p, ya, yr, gates, woa, wout, npl, wpg, bpg, wple)


def _rope_tables(s):
    pos = jnp.arange(s, dtype=jnp.float32)
    inv_freq = ROPE_THETA ** (-jnp.arange(0, HEAD_DIM, 2, dtype=jnp.float32) / HEAD_DIM)
    ang = pos[:, None] * inv_freq[None, :]
    cos, sin = jnp.cos(ang), jnp.sin(ang)
    cos2 = jnp.concatenate([cos, cos], axis=1)
    sin2 = jnp.concatenate([-sin, sin], axis=1)
    tabs = []
    for _, dil in ATTN_PATTERNS:
        if dil == 1:
            tabs.append((cos2, sin2))
        else:
            perm = lambda t: t.reshape(s // dil, dil, HEAD_DIM).transpose(1, 0, 2)
            tabs.append((perm(cos2), perm(sin2)))
    return tabs


def kernel(x, p, norm_mix, w_in, b_in, conv_w, conv_b, w_rg_a, b_rg_a, w_rg_x, b_rg_x,
           lru_lambda, q_norm, k_norm, w_o_rnn, w_o_att, w_out, norm_ple, w_ple_gate,
           b_ple_gate, w_ple):
    depth = w_in.shape[0]
    s = x.shape[1]
    tabs = _rope_tables(s)
    f32 = jnp.float32
    for layer in range(depth):
        w = w_in[layer]
        b = b_in[layer].astype(f32)[None, :]
        nm = norm_mix[layer].astype(f32)[None, :]
        natural = lambda a: jnp.concatenate(
            [a[:, OFF_QKV:OFF_QKV + GROUP_W], a[:, OFF_Z_ATT:]], axis=1)
        grp = lambda a, g: a[:, OFF_QKV + g * GROUP_W:OFF_QKV + (g + 1) * GROUP_W]
        gq = q_norm[layer].astype(f32) * (HEAD_DIM ** -0.5)
        gk = k_norm[layer].astype(f32)
        qkv0, qkv1, qkv2, sz, gates = _in_proj_attn(
            x, nm,
            natural(w).astype(MM_DTYPE), natural(b),
            grp(w, 1).astype(MM_DTYPE), grp(b, 1),
            grp(w, 2).astype(MM_DTYPE), grp(b, 2),
            gq, gk, tabs)
        wg = jnp.concatenate([w_rg_a[layer], w_rg_x[layer]], axis=2).astype(MM_DTYPE)
        bg = jnp.stack([b_rg_a[layer], b_rg_x[layer]], axis=0).astype(f32)
        yr = _rnn_branch(
            x, nm, w[:, :OFF_QKV].astype(MM_DTYPE), b[:, :OFF_QKV],
            conv_w[layer].astype(f32), conv_b[layer].astype(f32)[None, :], wg, bg,
            lru_lambda[layer].astype(f32)[None, :], w_o_rnn[layer].astype(MM_DTYPE))
        ya = _attention(qkv0, qkv1, qkv2, sz)
        x = _out_proj(
            x, p[layer], ya, yr, gates,
            w_o_att[layer].astype(MM_DTYPE), w_out[layer].astype(MM_DTYPE),
            norm_ple[layer].astype(f32)[None, :], w_ple_gate[layer].astype(MM_DTYPE),
            b_ple_gate[layer].astype(f32)[None, :], w_ple[layer].astype(MM_DTYPE))
    return x
```

```python
import functools

import jax
import jax.numpy as jnp
from jax import lax
from jax.experimental import pallas as pl
from jax.experimental.pallas import tpu as pltpu

D_MODEL = 1024
PLE_DIM = 256
D_RNN = 1280
RNN_BLOCKS = 10
RNN_BLOCK_W = D_RNN // RNN_BLOCKS
CONV_W = 4
LRU_C = 8.0
HEAD_DIM = 128
HEADS = 4
ATTN_PATTERNS = ((128, 1), (512, 4), (2048, 16))
N_GROUPS = len(ATTN_PATTERNS)
ATT_W = HEADS * HEAD_DIM
GROUP_W = 3 * ATT_W
ATT_BLK = 128
ROPE_THETA = 10000.0
EPS = 1e-6

OFF_Z_RNN = D_RNN
OFF_QKV = 2 * D_RNN
OFF_Z_ATT = OFF_QKV + N_GROUPS * GROUP_W
OFF_GATES = OFF_Z_ATT + ATT_W

LANES = 128
SUBLANES = 8
VMEM_LIMIT_BYTES = 56 * 1024 * 1024

TM_IN = 256
TC_RNN = 256
TM_OUT = 512
COMBINE_ROWS = 256

MM_DTYPE = jnp.bfloat16
NEG_INF = float("-inf")


def _sigmoid(v):
    return 1.0 / (1.0 + jnp.exp(-v))


def _rms_normalize(v):
    var = jnp.mean(v * v, axis=-1, keepdims=True)
    return v * lax.rsqrt(var + EPS)


def _dot(a, b):
    return jnp.dot(a, b, preferred_element_type=jnp.float32)


def _const_spec(shape):
    zeros = (0,) * len(shape)
    return pl.BlockSpec(shape, lambda *_: zeros, pipeline_mode=pl.Buffered(1))


def _params(n_grid):
    return pltpu.CompilerParams(
        dimension_semantics=("arbitrary",) * n_grid,
        vmem_limit_bytes=VMEM_LIMIT_BYTES,
    )


def _qk_epilogue(acc, gq, gk, cos2, sin2):
    outs = []
    for part, gain in ((0, gq), (1, gk)):
        for h in range(HEADS):
            lo = part * ATT_W + h * HEAD_DIM
            t = _rms_normalize(acc[:, lo:lo + HEAD_DIM]) * gain
            outs.append(t * cos2 + pltpu.roll(t, HEAD_DIM // 2, 1) * sin2)
    outs.append(acc[:, 2 * ATT_W:])
    return jnp.concatenate(outs, axis=1)


def _in_proj_attn_kernel(x_ref, nm_ref, w0_ref, b0_ref, w1_ref, b1_ref, w2_ref, b2_ref,
                         gq_ref, gk_ref, cos0_ref, sin0_ref, cos1_ref, sin1_ref,
                         cos2_ref, sin2_ref,
                         qkv0_ref, qkv1_ref, qkv2_ref, sz_ref, gates_ref, hs_ref):
    tm = x_ref.shape[0]
    n_slabs = D_MODEL // LANES
    hn = _rms_normalize(x_ref[...]) * nm_ref[...]

    hb = hn.astype(MM_DTYPE)
    acc = _dot(hb, w0_ref[:, :GROUP_W]) + b0_ref[:, :GROUP_W]
    qkv0_ref[...] = _qk_epilogue(acc, gq_ref[0:1], gk_ref[0:1], cos0_ref[...],
                                 sin0_ref[...]).astype(qkv0_ref.dtype)
    z = _dot(hb, w0_ref[:, GROUP_W:GROUP_W + ATT_W]) + b0_ref[:, GROUP_W:GROUP_W + ATT_W]
    sz_ref[...] = (z * _sigmoid(z)).astype(sz_ref.dtype)
    g = _dot(hb, w0_ref[:, GROUP_W + ATT_W:]) + b0_ref[:, GROUP_W + ATT_W:]
    gates_ref[...] = _sigmoid(g).astype(gates_ref.dtype)

    for j in range(n_slabs):
        hs_ref[j] = hn[:, j * LANES:(j + 1) * LANES]
    for (dil, w_ref, b_ref, gi, cos_ref, sin_ref, out_ref) in (
            (ATTN_PATTERNS[1][1], w1_ref, b1_ref, 1, cos1_ref, sin1_ref, qkv1_ref),
            (ATTN_PATTERNS[2][1], w2_ref, b2_ref, 2, cos2_ref, sin2_ref, qkv2_ref)):
        per = tm // dil
        classes = []
        for c in range(dil):
            classes.append(jnp.concatenate(
                [hs_ref[j, pl.ds(c, per, stride=dil), :] for j in range(n_slabs)], axis=1))
        hp = jnp.concatenate(classes, axis=0).astype(MM_DTYPE)
        acc = _dot(hp, w_ref[...]) + b_ref[...]
        cos2 = cos_ref[...].reshape(tm, HEAD_DIM)
        sin2 = sin_ref[...].reshape(tm, HEAD_DIM)
        res = _qk_epilogue(acc, gq_ref[gi:gi + 1], gk_ref[gi:gi + 1], cos2, sin2)
        out_ref[...] = res.astype(out_ref.dtype).reshape(out_ref.shape)


def _in_proj_attn(x, nm, w0, b0, w1, b1, w2, b2, gq, gk, tabs):
    B, S, _ = x.shape
    tm = TM_IN
    d1, d2 = ATTN_PATTERNS[1][1], ATTN_PATTERNS[2][1]
    (cos0, sin0), (cos1, sin1), (cos2, sin2) = tabs
    grid = (B, S // tm)
    row = lambda w: pl.BlockSpec((None, tm, w), lambda b, i: (b, i, 0))
    in_specs = [
        row(D_MODEL),
        _const_spec(nm.shape),
        _const_spec(w0.shape), _const_spec(b0.shape),
        _const_spec(w1.shape), _const_spec(b1.shape),
        _const_spec(w2.shape), _const_spec(b2.shape),
        _const_spec(gq.shape), _const_spec(gk.shape),
        pl.BlockSpec((tm, HEAD_DIM), lambda b, i: (i, 0)),
        pl.BlockSpec((tm, HEAD_DIM), lambda b, i: (i, 0)),
        pl.BlockSpec((d1, tm // d1, HEAD_DIM), lambda b, i: (0, i, 0)),
        pl.BlockSpec((d1, tm // d1, HEAD_DIM), lambda b, i: (0, i, 0)),
        pl.BlockSpec((d2, tm // d2, HEAD_DIM), lambda b, i: (0, i, 0)),
        pl.BlockSpec((d2, tm // d2, HEAD_DIM), lambda b, i: (0, i, 0)),
    ]
    out_shape = (
        jax.ShapeDtypeStruct((B, S, GROUP_W), MM_DTYPE),
        jax.ShapeDtypeStruct((B, d1, S // d1, GROUP_W), MM_DTYPE),
        jax.ShapeDtypeStruct((B, d2, S // d2, GROUP_W), MM_DTYPE),
        jax.ShapeDtypeStruct((B, S, ATT_W), MM_DTYPE),
        jax.ShapeDtypeStruct((B, S, 2 * D_MODEL), MM_DTYPE),
    )
    out_specs = (
        row(GROUP_W),
        pl.BlockSpec((None, d1, tm // d1, GROUP_W), lambda b, i: (b, 0, i, 0)),
        pl.BlockSpec((None, d2, tm // d2, GROUP_W), lambda b, i: (b, 0, i, 0)),
        row(ATT_W),
        row(2 * D_MODEL),
    )
    return pl.pallas_call(
        _in_proj_attn_kernel,
        grid=grid,
        in_specs=in_specs,
        out_specs=out_specs,
        out_shape=out_shape,
        scratch_shapes=[pltpu.VMEM((D_MODEL // LANES, tm, LANES), jnp.float32)],
        compiler_params=_params(2),
        name="in_proj_attn",
    )(x, nm, w0, b0, w1, b1, w2, b2, gq, gk, cos0, sin0, cos1, sin1, cos2, sin2)


def _rnn_kernel(x_ref, nm_ref, wr_ref, br_ref, cw_ref, cb_ref, wg_ref, bg_ref, lam_ref,
                wo_ref, yr_ref, xs_ref, h_ref):
    tc = x_ref.shape[0]
    t_idx = pl.program_id(1)

    @pl.when(t_idx == 0)
    def _():
        xs_ref[0:SUBLANES, :] = jnp.zeros((SUBLANES, D_RNN), jnp.float32)
        h_ref[...] = jnp.zeros_like(h_ref)

    hb = (_rms_normalize(x_ref[...]) * nm_ref[...]).astype(MM_DTYPE)
    xz = _dot(hb, wr_ref[...]) + br_ref[...]
    z = xz[:, D_RNN:]
    xs_ref[SUBLANES:, :] = xz[:, :D_RNN]
    xc = cb_ref[...]
    for k in range(CONV_W):
        off = SUBLANES - (CONV_W - 1) + k
        xc = xc + cw_ref[k:k + 1, :] * xs_ref[pl.ds(off, tc), :]
    xs_ref[0:SUBLANES, :] = xs_ref[tc:tc + SUBLANES, :]

    xcb = xc.astype(MM_DTYPE)
    pre = [_dot(xcb[:, n * RNN_BLOCK_W:(n + 1) * RNN_BLOCK_W], wg_ref[n])
           for n in range(RNN_BLOCKS)]
    r = _sigmoid(jnp.concatenate([t[:, :RNN_BLOCK_W] for t in pre], axis=1) + bg_ref[0:1, :])
    gi = _sigmoid(jnp.concatenate([t[:, RNN_BLOCK_W:] for t in pre], axis=1) + bg_ref[1:2, :])
    neg_lam = -lam_ref[...]
    softplus = jnp.maximum(neg_lam, 0.0) + jnp.log(1.0 + jnp.exp(-jnp.abs(neg_lam)))
    a = jnp.exp((-LRU_C) * softplus * r)
    mult = jnp.sqrt(1.0 - a * a)
    rows = lax.broadcasted_iota(jnp.int32, (tc, 1), 0) + t_idx * tc
    mult = jnp.where(rows == 0, 1.0, mult)
    u = mult * (gi * xc)

    n_grp = tc // SUBLANES
    a3 = a.reshape(n_grp, SUBLANES, D_RNN)
    u3 = u.reshape(n_grp, SUBLANES, D_RNN)
    sub = lax.broadcasted_iota(jnp.int32, (1, SUBLANES, 1), 1)
    s = 1
    while s < SUBLANES:
        keep = sub >= s
        u_sh = jnp.where(keep, pltpu.roll(u3, s, 1), 0.0)
        a_sh = jnp.where(keep, pltpu.roll(a3, s, 1), 1.0)
        u3 = a3 * u_sh + u3
        a3 = a3 * a_sh
        s *= 2
    carry = h_ref[...]
    hs = []
    for j in range(n_grp):
        hj = u3[j] + a3[j] * carry
        hs.append(hj)
        carry = jnp.broadcast_to(hj[SUBLANES - 1:SUBLANES, :], (SUBLANES, D_RNN))
    h_ref[...] = carry
    h = jnp.concatenate(hs, axis=0)

    y = (h * (z * _sigmoid(z))).astype(MM_DTYPE)
    yr_ref[...] = _dot(y, wo_ref[...]).astype(yr_ref.dtype)


def _rnn_branch(x, nm, wr, br, cw, cb, wg, bg, lam, wo):
    B, S, _ = x.shape
    tc = TC_RNN
    return pl.pallas_call(
        _rnn_kernel,
        grid=(B, S // tc),
        in_specs=[
            pl.BlockSpec((None, tc, D_MODEL), lambda b, i: (b, i, 0)),
            _const_spec(nm.shape), _const_spec(wr.shape), _const_spec(br.shape),
            _const_spec(cw.shape), _const_spec(cb.shape), _const_spec(wg.shape),
            _const_spec(bg.shape), _const_spec(lam.shape), _const_spec(wo.shape),
        ],
        out_specs=pl.BlockSpec((None, tc, D_MODEL), lambda b, i: (b, i, 0)),
        out_shape=jax.ShapeDtypeStruct((B, S, D_MODEL), MM_DTYPE),
        scratch_shapes=[
            pltpu.VMEM((tc + SUBLANES, D_RNN), jnp.float32),
            pltpu.VMEM((SUBLANES, D_RNN), jnp.float32),
        ],
        compiler_params=_params(2),
        name="rnn_branch",
    )(x, nm, wr, br, cw, cb, wg, bg, lam, wo)


def _attn_kernel(q0, k0, v0, q1, k1, v1, q2, k2, v2, sz_ref, y_ref, o_nat, l_nat):
    S = sz_ref.shape[0]
    n_blk = S // ATT_BLK
    blk3 = (n_blk, ATT_BLK, HEAD_DIM)
    qi = lax.broadcasted_iota(jnp.int32, (1, ATT_BLK, ATT_BLK), 1)
    kj = lax.broadcasted_iota(jnp.int32, (1, ATT_BLK, ATT_BLK), 2)
    cur_ok = kj <= qi
    blk_id = lax.broadcasted_iota(jnp.int32, (n_blk, 1, 1), 0)
    qk_dims = (((2,), (2,)), ((0,), (0,)))
    pv_dims = (((2,), (1,)), ((0,), (0,)))

    for g, (q_ref, k_ref, v_ref) in enumerate(((q0, k0, v0), (q1, k1, v1), (q2, k2, v2))):
        dil = ATTN_PATTERNS[g][1]
        blocks_per_class = n_blk // dil
        q = q_ref[...].reshape(blk3)
        k = k_ref[...].reshape(blk3)
        v = v_ref[...].reshape(blk3)
        if blocks_per_class > 1:
            shift = lambda t: jnp.concatenate([t[:1], t[:-1]], axis=0)
            kk = jnp.concatenate([shift(k), k], axis=1)
            vv = jnp.concatenate([shift(v), v], axis=1)
            s = lax.dot_general(q, kk, qk_dims, preferred_element_type=jnp.float32)
            first = (blk_id & (blocks_per_class - 1)) == 0
            prev_ok = kj >= qi + jnp.where(first, ATT_BLK, 0)
            s_prev = jnp.where(prev_ok, s[:, :, :ATT_BLK], NEG_INF)
            s_cur = jnp.where(cur_ok, s[:, :, ATT_BLK:], NEG_INF)
            m = jnp.maximum(jnp.max(s_prev, axis=-1, keepdims=True),
                            jnp.max(s_cur, axis=-1, keepdims=True))
            e_prev = jnp.exp(s_prev - m)
            e_cur = jnp.exp(s_cur - m)
            den = (jnp.sum(e_prev, axis=-1, keepdims=True)
                   + jnp.sum(e_cur, axis=-1, keepdims=True))
            e = jnp.concatenate([e_prev, e_cur], axis=2)
        else:
            vv = v
            s = lax.dot_general(q, k, qk_dims, preferred_element_type=jnp.float32)
            s = jnp.where(cur_ok, s, NEG_INF)
            m = jnp.max(s, axis=-1, keepdims=True)
            e = jnp.exp(s - m)
            den = jnp.sum(e, axis=-1, keepdims=True)
        o = lax.dot_general(e.astype(MM_DTYPE), vv, pv_dims,
                            preferred_element_type=jnp.float32) * (1.0 / den)
        lse = jnp.broadcast_to(m + jnp.log(den), blk3)
        if dil == 1:
            o_nat[g] = o.reshape(S, HEAD_DIM)
            l_nat[g] = lse.reshape(S, HEAD_DIM)
        else:
            for nb in range(n_blk):
                c, m0 = divmod(nb, blocks_per_class)
                idx = pl.ds(m0 * ATT_BLK * dil + c, ATT_BLK, stride=dil)
                o_nat[g, idx, :] = o[nb]
                l_nat[g, idx, :] = lse[nb]

    def merge(i, carry):
        rows = pl.ds(pl.multiple_of(i * COMBINE_ROWS, COMBINE_ROWS), COMBINE_ROWS)
        ls = [l_nat[g, rows, :] for g in range(N_GROUPS)]
        m = functools.reduce(jnp.maximum, ls)
        ws = [jnp.exp(l - m) for l in ls]
        den = functools.reduce(lambda p, q: p + q, ws)
        att = functools.reduce(
            lambda p, q: p + q, [w * o_nat[g, rows, :] for g, w in enumerate(ws)])
        att = att * (1.0 / den)
        y_ref[rows, :] = (att * sz_ref[rows, :].astype(jnp.float32)).astype(y_ref.dtype)
        return carry
    lax.fori_loop(0, S // COMBINE_ROWS, merge, 0)


def _attention(qkv0, qkv1, qkv2, sz):
    B, S, _ = qkv0.shape
    qkv1 = qkv1.reshape(B, S, GROUP_W)
    qkv2 = qkv2.reshape(B, S, GROUP_W)
    in_specs = []
    for _ in range(N_GROUPS):
        for part in range(3):
            in_specs.append(pl.BlockSpec((None, S, HEAD_DIM),
                                         lambda b, h, part=part: (b, 0, part * HEADS + h)))
    in_specs.append(pl.BlockSpec((None, S, HEAD_DIM), lambda b, h: (b, 0, h)))
    return pl.pallas_call(
        _attn_kernel,
        grid=(B, HEADS),
        in_specs=in_specs,
        out_specs=pl.BlockSpec((None, S, HEAD_DIM), lambda b, h: (b, 0, h)),
        out_shape=jax.ShapeDtypeStruct((B, S, ATT_W), MM_DTYPE),
        scratch_shapes=[
            pltpu.VMEM((N_GROUPS, S, HEAD_DIM), jnp.float32),
            pltpu.VMEM((N_GROUPS, S, HEAD_DIM), jnp.float32),
        ],
        compiler_params=_params(2),
        name="attention",
    )(qkv0, qkv0, qkv0, qkv1, qkv1, qkv1, qkv2, qkv2, qkv2, sz)


def _out_kernel(x_ref, p_ref, ya_ref, yr_ref, g_ref, woa_ref, wout_ref, np_ref, wpg_ref,
                bpg_ref, wple_ref, o_ref):
    ya = _dot(ya_ref[...], woa_ref[...])
    g0 = g_ref[:, :D_MODEL].astype(jnp.float32)
    g1 = g_ref[:, D_MODEL:].astype(jnp.float32)
    merged = g0 * yr_ref[...].astype(jnp.float32) + g1 * ya
    x2 = x_ref[...] + _dot(merged.astype(MM_DTYPE), wout_ref[...])
    pe = _dot(p_ref[...].astype(MM_DTYPE), wple_ref[...])
    n2 = (_rms_normalize(x2) * np_ref[...]).astype(MM_DTYPE)
    pg = _sigmoid(_dot(n2, wpg_ref[...]) + bpg_ref[...])
    o_ref[...] = x2 + pg * pe


def _out_proj(x, p, ya, yr, gates, woa, wout, npl, wpg, bpg, wple):
    B, S, _ = x.shape
    tm = TM_OUT
    row = lambda w: pl.BlockSpec((None, tm, w), lambda b, i: (b, i, 0))
    return pl.pallas_call(
        _out_kernel,
        grid=(B, S // tm),
        in_specs=[
            row(D_MODEL), row(PLE_DIM), row(ATT_W), row(D_MODEL), row(2 * D_MODEL),
            _const_spec(woa.shape), _const_spec(wout.shape), _const_spec(npl.shape),
            _const_spec(wpg.shape), _const_spec(bpg.shape), _const_spec(wple.shape),
        ],
        out_specs=row(D_MODEL),
        out_shape=jax.ShapeDtypeStruct((B, S, D_MODEL), x.dtype),
        compiler_params=_params(2),
        name="out_proj",
    )(x, p, ya, yr, gates, woa, wout, npl, wpg, bpg, wple)


def _rope_tables(s):
    pos = jnp.arange(s, dtype=jnp.float32)
    inv_freq = ROPE_THETA ** (-jnp.arange(0, HEAD_DIM, 2, dtype=jnp.float32) / HEAD_DIM)
    ang = pos[:, None] * inv_freq[None, :]
    cos, sin = jnp.cos(ang), jnp.sin(ang)
    cos2 = jnp.concatenate([cos, cos], axis=1)
    sin2 = jnp.concatenate([-sin, sin], axis=1)
    tabs = []
    for _, dil in ATTN_PATTERNS:
        if dil == 1:
            tabs.append((cos2, sin2))
        else:
            perm = lambda t: t.reshape(s // dil, dil, HEAD_DIM).transpose(1, 0, 2)
            tabs.append((perm(cos2), perm(sin2)))
    return tabs


def kernel(x, p, norm_mix, w_in, b_in, conv_w, conv_b, w_rg_a, b_rg_a, w_rg_x, b_rg_x,
           lru_lambda, q_norm, k_norm, w_o_rnn, w_o_att, w_out, norm_ple, w_ple_gate,
           b_ple_gate, w_ple):
    depth = w_in.shape[0]
    s = x.shape[1]
    tabs = _rope_tables(s)
    f32 = jnp.float32
    for layer in range(depth):
        w = w_in[layer]
        b = b_in[layer].astype(f32)[None, :]
        nm = norm_mix[layer].astype(f32)[None, :]
        natural = lambda a: jnp.concatenate(
            [a[:, OFF_QKV:OFF_QKV + GROUP_W], a[:, OFF_Z_ATT:]], axis=1)
        grp = lambda a, g: a[:, OFF_QKV + g * GROUP_W:OFF_QKV + (g + 1) * GROUP_W]
        gq = q_norm[layer].astype(f32) * (HEAD_DIM ** -0.5)
        gk = k_norm[layer].astype(f32)
        qkv0, qkv1, qkv2, sz, gates = _in_proj_attn(
            x, nm,
            natural(w).astype(MM_DTYPE), natural(b),
            grp(w, 1).astype(MM_DTYPE), grp(b, 1),
            grp(w, 2).astype(MM_DTYPE), grp(b, 2),
            gq, gk, tabs)
        wg = jnp.concatenate([w_rg_a[layer], w_rg_x[layer]], axis=2).astype(MM_DTYPE)
        bg = jnp.stack([b_rg_a[layer], b_rg_x[layer]], axis=0).astype(f32)
        yr = _rnn_branch(
            x, nm, w[:, :OFF_QKV].astype(MM_DTYPE), b[:, :OFF_QKV],
            conv_w[layer].astype(f32), conv_b[layer].astype(f32)[None, :], wg, bg,
            lru_lambda[layer].astype(f32)[None, :], w_o_rnn[layer].astype(MM_DTYPE))
        ya = _attention(qkv0, qkv1, qkv2, sz)
        x = _out_proj(
            x, p[layer], ya, yr, gates,
            w_o_att[layer].astype(MM_DTYPE), w_out[layer].astype(MM_DTYPE),
            norm_ple[layer].astype(f32)[None, :], w_ple_gate[layer].astype(MM_DTYPE),
            b_ple_gate[layer].astype(f32)[None, :], w_ple[layer].astype(MM_DTYPE))
    return x
```

```python
import functools

import jax
import jax.numpy as jnp
from jax import lax
from jax.experimental import pallas as pl
from jax.experimental.pallas import tpu as pltpu

D_MODEL = 1024
PLE_DIM = 256
D_RNN = 1280
RNN_BLOCKS = 10
RNN_BLOCK_W = D_RNN // RNN_BLOCKS
CONV_W = 4
LRU_C = 8.0
HEAD_DIM = 128
HEADS = 4
ATTN_PATTERNS = ((128, 1), (512, 4), (2048, 16))
N_GROUPS = len(ATTN_PATTERNS)
ATT_W = HEADS * HEAD_DIM
GROUP_W = 3 * ATT_W
ATT_BLK = 128
ROPE_THETA = 10000.0
EPS = 1e-6

OFF_Z_RNN = D_RNN
OFF_QKV = 2 * D_RNN
OFF_Z_ATT = OFF_QKV + N_GROUPS * GROUP_W
OFF_GATES = OFF_Z_ATT + ATT_W

LANES = 128
SUBLANES = 8
VMEM_LIMIT_BYTES = 56 * 1024 * 1024

TM_IN = 256
TT_RNN = 32
RNN_SUB_T = 16
RNN_PITCH = 24
TM_OUT = 512
COMBINE_ROWS = 256

MM_DTYPE = jnp.bfloat16
NEG_INF = float("-inf")


def _sigmoid(v):
    return 1.0 / (1.0 + jnp.exp(-v))


def _rms_normalize(v):
    var = jnp.mean(v * v, axis=-1, keepdims=True)
    return v * lax.rsqrt(var + EPS)


def _dot(a, b):
    return jnp.dot(a, b, preferred_element_type=jnp.float32)


def _const_spec(shape):
    zeros = (0,) * len(shape)
    return pl.BlockSpec(shape, lambda *_: zeros, pipeline_mode=pl.Buffered(1))


def _params(n_grid):
    return pltpu.CompilerParams(
        dimension_semantics=("arbitrary",) * n_grid,
        vmem_limit_bytes=VMEM_LIMIT_BYTES,
    )


def _qk_epilogue(acc, gq, gk, cos2, sin2):
    outs = []
    for part, gain in ((0, gq), (1, gk)):
        for h in range(HEADS):
            lo = part * ATT_W + h * HEAD_DIM
            t = _rms_normalize(acc[:, lo:lo + HEAD_DIM]) * gain
            outs.append(t * cos2 + pltpu.roll(t, HEAD_DIM // 2, 1) * sin2)
    outs.append(acc[:, 2 * ATT_W:])
    return jnp.concatenate(outs, axis=1)


def _attn_natural(hb, wq0_ref, bq0_ref, wz_ref, bz_ref, wgt_ref, bgt_ref, gq_ref, gk_ref,
                  cos0_ref, sin0_ref, qkv0_ref, sz_ref, gates_ref):
    acc = _dot(hb, wq0_ref[...]) + bq0_ref[...]
    qkv0_ref[...] = _qk_epilogue(acc, gq_ref[0:1], gk_ref[0:1], cos0_ref[...],
                                 sin0_ref[...]).astype(qkv0_ref.dtype)
    z = _dot(hb, wz_ref[...]) + bz_ref[...]
    sz_ref[...] = (z * _sigmoid(z)).astype(sz_ref.dtype)
    g = _dot(hb, wgt_ref[...]) + bgt_ref[...]
    gates_ref[...] = _sigmoid(g).astype(gates_ref.dtype)


def _attn_dilated(hn, w1_ref, b1_ref, w2_ref, b2_ref, gq_ref, gk_ref, cos1_ref, sin1_ref,
                  cos2_ref, sin2_ref, qkv1_ref, qkv2_ref, hs_ref):
    tm = hn.shape[0]
    n_slabs = D_MODEL // LANES
    for j in range(n_slabs):
        hs_ref[j] = hn[:, j * LANES:(j + 1) * LANES]
    for (dil, w_ref, b_ref, gi, cos_ref, sin_ref, out_ref) in (
            (ATTN_PATTERNS[1][1], w1_ref, b1_ref, 1, cos1_ref, sin1_ref, qkv1_ref),
            (ATTN_PATTERNS[2][1], w2_ref, b2_ref, 2, cos2_ref, sin2_ref, qkv2_ref)):
        per = tm // dil
        classes = []
        for c in range(dil):
            classes.append(jnp.concatenate(
                [hs_ref[j, pl.ds(c, per, stride=dil), :] for j in range(n_slabs)], axis=1))
        hp = jnp.concatenate(classes, axis=0).astype(MM_DTYPE)
        acc = _dot(hp, w_ref[...]) + b_ref[...]
        cos2 = cos_ref[...].reshape(tm, HEAD_DIM)
        sin2 = sin_ref[...].reshape(tm, HEAD_DIM)
        res = _qk_epilogue(acc, gq_ref[gi:gi + 1], gk_ref[gi:gi + 1], cos2, sin2)
        out_ref[...] = res.astype(out_ref.dtype).reshape(out_ref.shape)


def _in_proj_attn_kernel(x_ref, nm_ref,
                         wq0_ref, bq0_ref, wz_ref, bz_ref, wgt_ref, bgt_ref, w1_ref, b1_ref,
                         w2_ref, b2_ref, gq_ref, gk_ref,
                         cos0_ref, sin0_ref, cos1_ref, sin1_ref, cos2_ref, sin2_ref,
                         qkv0_ref, qkv1_ref, qkv2_ref, sz_ref, gates_ref, hs_ref):
    hn = _rms_normalize(x_ref[...]) * nm_ref[...]
    _attn_natural(hn.astype(MM_DTYPE), wq0_ref, bq0_ref, wz_ref, bz_ref, wgt_ref, bgt_ref,
                  gq_ref, gk_ref, cos0_ref, sin0_ref, qkv0_ref, sz_ref, gates_ref)
    _attn_dilated(hn, w1_ref, b1_ref, w2_ref, b2_ref, gq_ref, gk_ref, cos1_ref, sin1_ref,
                  cos2_ref, sin2_ref, qkv1_ref, qkv2_ref, hs_ref)


def _in_proj_attn(x, nm, attn_w, tabs):
    B, S, _ = x.shape
    tm = TM_IN
    d1, d2 = ATTN_PATTERNS[1][1], ATTN_PATTERNS[2][1]
    (cos0, sin0), (cos1, sin1), (cos2, sin2) = tabs
    row = lambda w: pl.BlockSpec((None, tm, w), lambda b, i: (b, i, 0))
    tab_specs = [
        pl.BlockSpec((tm, HEAD_DIM), lambda b, i: (i, 0)),
        pl.BlockSpec((tm, HEAD_DIM), lambda b, i: (i, 0)),
        pl.BlockSpec((d1, tm // d1, HEAD_DIM), lambda b, i: (0, i, 0)),
        pl.BlockSpec((d1, tm // d1, HEAD_DIM), lambda b, i: (0, i, 0)),
        pl.BlockSpec((d2, tm // d2, HEAD_DIM), lambda b, i: (0, i, 0)),
        pl.BlockSpec((d2, tm // d2, HEAD_DIM), lambda b, i: (0, i, 0)),
    ]
    in_specs = ([row(D_MODEL), _const_spec(nm.shape)]
                + [_const_spec(a.shape) for a in attn_w] + tab_specs)
    out_shape = (
        jax.ShapeDtypeStruct((B, S, GROUP_W), MM_DTYPE),
        jax.ShapeDtypeStruct((B, d1, S // d1, GROUP_W), MM_DTYPE),
        jax.ShapeDtypeStruct((B, d2, S // d2, GROUP_W), MM_DTYPE),
        jax.ShapeDtypeStruct((B, S, ATT_W), MM_DTYPE),
        jax.ShapeDtypeStruct((B, S, 2 * D_MODEL), MM_DTYPE),
    )
    out_specs = (
        row(GROUP_W),
        pl.BlockSpec((None, d1, tm // d1, GROUP_W), lambda b, i: (b, 0, i, 0)),
        pl.BlockSpec((None, d2, tm // d2, GROUP_W), lambda b, i: (b, 0, i, 0)),
        row(ATT_W),
        row(2 * D_MODEL),
    )
    return pl.pallas_call(
        _in_proj_attn_kernel,
        grid=(B, S // tm),
        in_specs=in_specs,
        out_specs=out_specs,
        out_shape=out_shape,
        scratch_shapes=[pltpu.VMEM((D_MODEL // LANES, tm, LANES), jnp.float32)],
        compiler_params=_params(2),
        name="in_proj_attn",
    )(x, nm, *attn_w, cos0, sin0, cos1, sin1, cos2, sin2)


def _rnn_kernel(x_ref, nm_ref, wx_ref, bx_ref, wz_ref, bz_ref, cw_ref, cb_ref, wg_ref, bg_ref,
                lam_ref, wo_ref, yr_ref, hs_ref, xs_ref, h_ref, ys_ref):
    nb, tt, _ = x_ref.shape
    ts = RNN_SUB_T
    n_sub = tt // ts
    rows = nb * ts
    n_slabs = D_MODEL // LANES
    hist = (CONV_W - 1) * nb
    step = pl.program_id(0)

    @pl.when(step == 0)
    def _():
        xs_ref[0:hist, :] = jnp.zeros((hist, D_RNN), jnp.float32)
        h_ref[...] = jnp.zeros_like(h_ref)

    staged = []
    for s in range(n_sub):
        for b in range(nb):
            hn = _rms_normalize(x_ref[b, s * ts:(s + 1) * ts, :]) * nm_ref[...]
            for j in range(n_slabs):
                hs_ref[s, j, pl.ds(b * RNN_PITCH, ts), :] = hn[:, j * LANES:(j + 1) * LANES]
        hp = jnp.concatenate(
            [jnp.concatenate([hs_ref[s, j, pl.ds(t, nb, stride=RNN_PITCH), :]
                              for j in range(n_slabs)], axis=1) for t in range(ts)],
            axis=0).astype(MM_DTYPE)
        xr = _dot(hp, wx_ref[...]) + bx_ref[...]
        z = _dot(hp, wz_ref[...]) + bz_ref[...]
        xs_ref[pl.ds(hist + s * rows, rows), :] = xr
        xc = cb_ref[...]
        for k in range(CONV_W):
            xc = xc + cw_ref[k:k + 1, :] * xs_ref[pl.ds(s * rows + k * nb, rows), :]
        xcb = xc.astype(MM_DTYPE)
        pre = [_dot(xcb[:, n * RNN_BLOCK_W:(n + 1) * RNN_BLOCK_W], wg_ref[n])
               for n in range(RNN_BLOCKS)]
        staged.append((xc, pre, z))

    neg_lam = -lam_ref[...]
    softplus = jnp.maximum(neg_lam, 0.0) + jnp.log(1.0 + jnp.exp(-jnp.abs(neg_lam)))
    log_a_scale = (-LRU_C) * softplus
    h = h_ref[...]
    for s in range(n_sub):
        xc, pre, z = staged[s]
        r = _sigmoid(jnp.concatenate([t[:, :RNN_BLOCK_W] for t in pre], axis=1)
                     + bg_ref[0:1, :])
        gi = _sigmoid(jnp.concatenate([t[:, RNN_BLOCK_W:] for t in pre], axis=1)
                      + bg_ref[1:2, :])
        a = jnp.exp(log_a_scale * r)
        y1 = 1.0 - a * a
        mult = jnp.where(y1 > 0.0, y1 * lax.rsqrt(y1), 0.0)
        if s == 0:
            row_id = lax.broadcasted_iota(jnp.int32, (rows, 1), 0) + step * (nb * tt)
            mult = jnp.where(row_id < nb, 1.0, mult)
        u = mult * (gi * xc)

        hs = []
        for t in range(ts):
            h = a[t * nb:(t + 1) * nb] * h + u[t * nb:(t + 1) * nb]
            hs.append(h)
        y = (jnp.concatenate(hs, axis=0) * (z * _sigmoid(z))).astype(MM_DTYPE)
        yr = _dot(y, wo_ref[...])

        for j in range(n_slabs):
            for t in range(ts):
                ys_ref[s, j, pl.ds(t * RNN_PITCH, nb), :] = yr[t * nb:(t + 1) * nb,
                                                               j * LANES:(j + 1) * LANES]
        for b in range(nb):
            yr_ref[b, s * ts:(s + 1) * ts, :] = jnp.concatenate(
                [ys_ref[s, j, pl.ds(b, ts, stride=RNN_PITCH), :] for j in range(n_slabs)],
                axis=1).astype(yr_ref.dtype)
    h_ref[...] = h
    xs_ref[0:hist, :] = xs_ref[pl.ds(n_sub * rows, hist), :]


def _rnn_branch(x, nm, rnn_w):
    B, S, _ = x.shape
    tt = TT_RNN
    n_slabs = D_MODEL // LANES
    return pl.pallas_call(
        _rnn_kernel,
        grid=(S // tt,),
        in_specs=([pl.BlockSpec((B, tt, D_MODEL), lambda i: (0, i, 0)), _const_spec(nm.shape)]
                  + [_const_spec(a.shape) for a in rnn_w]),
        out_specs=pl.BlockSpec((B, tt, D_MODEL), lambda i: (0, i, 0)),
        out_shape=jax.ShapeDtypeStruct((B, S, D_MODEL), MM_DTYPE),
        scratch_shapes=[
            pltpu.VMEM((tt // RNN_SUB_T, n_slabs, B * RNN_PITCH, LANES), jnp.float32),
            pltpu.VMEM(((CONV_W - 1) * B + B * tt, D_RNN), jnp.float32),
            pltpu.VMEM((B, D_RNN), jnp.float32),
            pltpu.VMEM((tt // RNN_SUB_T, n_slabs, RNN_SUB_T * RNN_PITCH, LANES), jnp.float32),
        ],
        compiler_params=_params(1),
        name="rnn_branch",
    )(x, nm, *rnn_w)


def _attn_kernel(q0, k0, v0, q1, k1, v1, q2, k2, v2, sz_ref, y_ref, o_nat, l_nat):
    S = sz_ref.shape[0]
    n_blk = S // ATT_BLK
    blk3 = (n_blk, ATT_BLK, HEAD_DIM)
    qi = lax.broadcasted_iota(jnp.int32, (1, ATT_BLK, ATT_BLK), 1)
    kj = lax.broadcasted_iota(jnp.int32, (1, ATT_BLK, ATT_BLK), 2)
    cur_ok = kj <= qi
    blk_id = lax.broadcasted_iota(jnp.int32, (n_blk, 1, 1), 0)
    qk_dims = (((2,), (2,)), ((0,), (0,)))
    pv_dims = (((2,), (1,)), ((0,), (0,)))

    for g, (q_ref, k_ref, v_ref) in enumerate(((q0, k0, v0), (q1, k1, v1), (q2, k2, v2))):
        dil = ATTN_PATTERNS[g][1]
        blocks_per_class = n_blk // dil
        q = q_ref[...].reshape(blk3)
        k = k_ref[...].reshape(blk3)
        v = v_ref[...].reshape(blk3)
        if blocks_per_class > 1:
            shift = lambda t: jnp.concatenate([t[:1], t[:-1]], axis=0)
            kk = jnp.concatenate([shift(k), k], axis=1)
            vv = jnp.concatenate([shift(v), v], axis=1)
            s = lax.dot_general(q, kk, qk_dims, preferred_element_type=jnp.float32)
            first = (blk_id & (blocks_per_class - 1)) == 0
            prev_ok = kj >= qi + jnp.where(first, ATT_BLK, 0)
            s_prev = jnp.where(prev_ok, s[:, :, :ATT_BLK], NEG_INF)
            s_cur = jnp.where(cur_ok, s[:, :, ATT_BLK:], NEG_INF)
            m = jnp.maximum(jnp.max(s_prev, axis=-1, keepdims=True),
                            jnp.max(s_cur, axis=-1, keepdims=True))
            e_prev = jnp.exp(s_prev - m)
            e_cur = jnp.exp(s_cur - m)
            den = (jnp.sum(e_prev, axis=-1, keepdims=True)
                   + jnp.sum(e_cur, axis=-1, keepdims=True))
            e = jnp.concatenate([e_prev, e_cur], axis=2)
        else:
            vv = v
            s = lax.dot_general(q, k, qk_dims, preferred_element_type=jnp.float32)
            s = jnp.where(cur_ok, s, NEG_INF)
            m = jnp.max(s, axis=-1, keepdims=True)
            e = jnp.exp(s - m)
            den = jnp.sum(e, axis=-1, keepdims=True)
        o = lax.dot_general(e.astype(MM_DTYPE), vv, pv_dims,
                            preferred_element_type=jnp.float32) * (1.0 / den)
        lse = jnp.broadcast_to(m + jnp.log(den), blk3)
        if dil == 1:
            o_nat[g] = o.reshape(S, HEAD_DIM)
            l_nat[g] = lse.reshape(S, HEAD_DIM)
        else:
            for nb in range(n_blk):
                c, m0 = divmod(nb, blocks_per_class)
                idx = pl.ds(m0 * ATT_BLK * dil + c, ATT_BLK, stride=dil)
                o_nat[g, idx, :] = o[nb]
                l_nat[g, idx, :] = lse[nb]

    def merge(i, carry):
        rows = pl.ds(pl.multiple_of(i * COMBINE_ROWS, COMBINE_ROWS), COMBINE_ROWS)
        ls = [l_nat[g, rows, :] for g in range(N_GROUPS)]
        m = functools.reduce(jnp.maximum, ls)
        ws = [jnp.exp(l - m) for l in ls]
        den = functools.reduce(lambda p, q: p + q, ws)
        att = functools.reduce(
            lambda p, q: p + q, [w * o_nat[g, rows, :] for g, w in enumerate(ws)])
        att = att * (1.0 / den)
        y_ref[rows, :] = (att * sz_ref[rows, :].astype(jnp.float32)).astype(y_ref.dtype)
        return carry
    lax.fori_loop(0, S // COMBINE_ROWS, merge, 0)


def _attention(qkv0, qkv1, qkv2, sz):
    B, S, _ = qkv0.shape
    qkv1 = qkv1.reshape(B, S, GROUP_W)
    qkv2 = qkv2.reshape(B, S, GROUP_W)
    in_specs = []
    for _ in range(N_GROUPS):
        for part in range(3):
            in_specs.append(pl.BlockSpec((None, S, HEAD_DIM),
                                         lambda b, h, part=part: (b, 0, part * HEADS + h)))
    in_specs.append(pl.BlockSpec((None, S, HEAD_DIM), lambda b, h: (b, 0, h)))
    return pl.pallas_call(
        _attn_kernel,
        grid=(B, HEADS),
        in_specs=in_specs,
        out_specs=pl.BlockSpec((None, S, HEAD_DIM), lambda b, h: (b, 0, h)),
        out_shape=jax.ShapeDtypeStruct((B, S, ATT_W), MM_DTYPE),
        scratch_shapes=[
            pltpu.VMEM((N_GROUPS, S, HEAD_DIM), jnp.float32),
            pltpu.VMEM((N_GROUPS, S, HEAD_DIM), jnp.float32),
        ],
        compiler_params=_params(2),
        name="attention",
    )(qkv0, qkv0, qkv0, qkv1, qkv1, qkv1, qkv2, qkv2, qkv2, sz)


def _out_kernel(x_ref, p_ref, ya_ref, yr_ref, g_ref, woa_ref, wout_ref, np_ref, wpg_ref,
                bpg_ref, wple_ref, o_ref):
    ya = _dot(ya_ref[...], woa_ref[...])
    g0 = g_ref[:, :D_MODEL].astype(jnp.float32)
    g1 = g_ref[:, D_MODEL:].astype(jnp.float32)
    merged = g0 * yr_ref[...].astype(jnp.float32) + g1 * ya
    x2 = x_ref[...] + _dot(merged.astype(MM_DTYPE), wout_ref[...])
    pe = _dot(p_ref[...].astype(MM_DTYPE), wple_ref[...])
    n2 = (_rms_normalize(x2) * np_ref[...]).astype(MM_DTYPE)
    pg = _sigmoid(_dot(n2, wpg_ref[...]) + bpg_ref[...])
    o_ref[...] = x2 + pg * pe


def _out_proj(x, p, ya, yr, gates, woa, wout, npl, wpg, bpg, wple):
    B, S, _ = x.shape
    tm = TM_OUT
    row = lambda w: pl.BlockSpec((None, tm, w), lambda b, i: (b, i, 0))
    return pl.pallas_call(
        _out_kernel,
        grid=(B, S // tm),
        in_specs=[
            row(D_MODEL), row(PLE_DIM), row(ATT_W), row(D_MODEL), row(2 * D_MODEL),
            _const_spec(woa.shape), _const_spec(wout.shape), _const_spec(npl.shape),
            _const_spec(wpg.shape), _const_spec(bpg.shape), _const_spec(wple.shape),
        ],
        out_specs=row(D_MODEL),
        out_shape=jax.ShapeDtypeStruct((B, S, D_MODEL), x.dtype),
        compiler_params=_params(2),
        name="out_proj",
    )(x, p, ya, yr, gates, woa, wout, npl, wpg, bpg, wple)


def _rope_tables(s):
    pos = jnp.arange(s, dtype=jnp.float32)
    inv_freq = ROPE_THETA ** (-jnp.arange(0, HEAD_DIM, 2, dtype=jnp.float32) / HEAD_DIM)
    ang = pos[:, None] * inv_freq[None, :]
    cos, sin = jnp.cos(ang), jnp.sin(ang)
    cos2 = jnp.concatenate([cos, cos], axis=1)
    sin2 = jnp.concatenate([-sin, sin], axis=1)
    tabs = []
    for _, dil in ATTN_PATTERNS:
        if dil == 1:
            tabs.append((cos2, sin2))
        else:
            perm = lambda t: t.reshape(s // dil, dil, HEAD_DIM).transpose(1, 0, 2)
            tabs.append((perm(cos2), perm(sin2)))
    return tabs


def kernel(x, p, norm_mix, w_in, b_in, conv_w, conv_b, w_rg_a, b_rg_a, w_rg_x, b_rg_x,
           lru_lambda, q_norm, k_norm, w_o_rnn, w_o_att, w_out, norm_ple, w_ple_gate,
           b_ple_gate, w_ple):
    depth = w_in.shape[0]
    s = x.shape[1]
    tabs = _rope_tables(s)
    f32 = jnp.float32
    for layer in range(depth):
        w = w_in[layer]
        b = b_in[layer].astype(f32)[None, :]
        nm = norm_mix[layer].astype(f32)[None, :]
        cols = lambda a, lo, hi: a[:, lo:hi]
        grp = lambda a, g: cols(a, OFF_QKV + g * GROUP_W, OFF_QKV + (g + 1) * GROUP_W)
        wb = lambda lo, hi: (cols(w, lo, hi).astype(MM_DTYPE), cols(b, lo, hi))
        gq = q_norm[layer].astype(f32) * (HEAD_DIM ** -0.5)
        gk = k_norm[layer].astype(f32)
        attn_w = (grp(w, 0).astype(MM_DTYPE), grp(b, 0),
                  *wb(OFF_Z_ATT, OFF_GATES), *wb(OFF_GATES, w.shape[1]),
                  grp(w, 1).astype(MM_DTYPE), grp(b, 1),
                  grp(w, 2).astype(MM_DTYPE), grp(b, 2), gq, gk)
        wg = jnp.concatenate([w_rg_a[layer], w_rg_x[layer]], axis=2).astype(MM_DTYPE)
        bg = jnp.stack([b_rg_a[layer], b_rg_x[layer]], axis=0).astype(f32)
        rnn_w = (*wb(0, OFF_Z_RNN), *wb(OFF_Z_RNN, OFF_QKV), conv_w[layer].astype(f32),
                 conv_b[layer].astype(f32)[None, :], wg, bg,
                 lru_lambda[layer].astype(f32)[None, :], w_o_rnn[layer].astype(MM_DTYPE))
        qkv0, qkv1, qkv2, sz, gates = _in_proj_attn(x, nm, attn_w, tabs)
        yr = _rnn_branch(x, nm, rnn_w)
        ya = _attention(qkv0, qkv1, qkv2, sz)
        x = _out_proj(
            x, p[layer], ya, yr, gates,
            w_o_att[layer].astype(MM_DTYPE), w_out[layer].astype(MM_DTYPE),
            norm_ple[layer].astype(f32)[None, :], w_ple_gate[layer].astype(MM_DTYPE),
            b_ple_gate[layer].astype(f32)[None, :], w_ple[layer].astype(MM_DTYPE))
    return x
```

```python
import functools

import jax
import jax.numpy as jnp
from jax import lax
from jax.experimental import pallas as pl
from jax.experimental.pallas import tpu as pltpu

D_MODEL = 1024
PLE_DIM = 256
D_RNN = 1280
RNN_BLOCKS = 10
RNN_BLOCK_W = D_RNN // RNN_BLOCKS
CONV_W = 4
LRU_C = 8.0
HEAD_DIM = 128
HEADS = 4
ATTN_PATTERNS = ((128, 1), (512, 4), (2048, 16))
N_GROUPS = len(ATTN_PATTERNS)
ATT_W = HEADS * HEAD_DIM
GROUP_W = 3 * ATT_W
ATT_BLK = 128
ROPE_THETA = 10000.0
EPS = 1e-6

OFF_Z_RNN = D_RNN
OFF_QKV = 2 * D_RNN
OFF_Z_ATT = OFF_QKV + N_GROUPS * GROUP_W
OFF_GATES = OFF_Z_ATT + ATT_W

LANES = 128
SUBLANES = 8
VMEM_LIMIT_BYTES = 56 * 1024 * 1024

TM_IN = 512
IN_SUB = 2
TT_RNN = 32
RNN_SUB_T = 16
RNN_PITCH = 24
TM_OUT = 1024
OUT_SUB = 2
COMBINE_ROWS = 256

MM_DTYPE = jnp.bfloat16
NEG_INF = float("-inf")


def _sigmoid(v):
    return 1.0 / (1.0 + jnp.exp(-v))


def _rms_normalize(v):
    var = jnp.mean(v * v, axis=-1, keepdims=True)
    return v * lax.rsqrt(var + EPS)


def _dot(a, b):
    return jnp.dot(a, b, preferred_element_type=jnp.float32)


def _const_spec(shape):
    zeros = (0,) * len(shape)
    return pl.BlockSpec(shape, lambda *_: zeros, pipeline_mode=pl.Buffered(1))


def _params(n_grid):
    return pltpu.CompilerParams(
        dimension_semantics=("arbitrary",) * n_grid,
        vmem_limit_bytes=VMEM_LIMIT_BYTES,
    )


def _qk_epilogue(acc, gq, gk, cos2, sin2):
    outs = []
    for part, gain in ((0, gq), (1, gk)):
        for h in range(HEADS):
            lo = part * ATT_W + h * HEAD_DIM
            t = _rms_normalize(acc[:, lo:lo + HEAD_DIM]) * gain
            outs.append(t * cos2 + pltpu.roll(t, HEAD_DIM // 2, 1) * sin2)
    outs.append(acc[:, 2 * ATT_W:])
    return jnp.concatenate(outs, axis=1)


def _attn_natural(hb, rows, wq0_ref, bq0_ref, wz_ref, bz_ref, wgt_ref, bgt_ref, gq_ref, gk_ref,
                  cos0_ref, sin0_ref, qkv0_ref, sz_ref, gates_ref):
    acc = _dot(hb, wq0_ref[...]) + bq0_ref[...]
    qkv0_ref[rows, :] = _qk_epilogue(acc, gq_ref[0:1], gk_ref[0:1], cos0_ref[rows, :],
                                     sin0_ref[rows, :]).astype(qkv0_ref.dtype)
    z = _dot(hb, wz_ref[...]) + bz_ref[...]
    sz_ref[rows, :] = (z * _sigmoid(z)).astype(sz_ref.dtype)
    g = _dot(hb, wgt_ref[...]) + bgt_ref[...]
    gates_ref[rows, :] = _sigmoid(g).astype(gates_ref.dtype)


def _attn_dilated(hn, sub, w1_ref, b1_ref, w2_ref, b2_ref, gq_ref, gk_ref, cos1_ref, sin1_ref,
                  cos2_ref, sin2_ref, qkv1_ref, qkv2_ref, hs_ref):
    tm = hn.shape[0]
    n_slabs = D_MODEL // LANES
    for j in range(n_slabs):
        hs_ref[sub, j] = hn[:, j * LANES:(j + 1) * LANES]
    for (dil, w_ref, b_ref, gi, cos_ref, sin_ref, out_ref) in (
            (ATTN_PATTERNS[1][1], w1_ref, b1_ref, 1, cos1_ref, sin1_ref, qkv1_ref),
            (ATTN_PATTERNS[2][1], w2_ref, b2_ref, 2, cos2_ref, sin2_ref, qkv2_ref)):
        per = tm // dil
        classes = []
        for c in range(dil):
            classes.append(jnp.concatenate(
                [hs_ref[sub, j, pl.ds(c, per, stride=dil), :] for j in range(n_slabs)],
                axis=1))
        hp = jnp.concatenate(classes, axis=0).astype(MM_DTYPE)
        acc = _dot(hp, w_ref[...]) + b_ref[...]
        part = pl.ds(sub * per, per)
        cos2 = cos_ref[:, part, :].reshape(tm, HEAD_DIM)
        sin2 = sin_ref[:, part, :].reshape(tm, HEAD_DIM)
        res = _qk_epilogue(acc, gq_ref[gi:gi + 1], gk_ref[gi:gi + 1], cos2, sin2)
        out_ref[:, part, :] = res.astype(out_ref.dtype).reshape(dil, per, GROUP_W)


def _in_proj_attn_kernel(x_ref, nm_ref,
                         wq0_ref, bq0_ref, wz_ref, bz_ref, wgt_ref, bgt_ref, w1_ref, b1_ref,
                         w2_ref, b2_ref, gq_ref, gk_ref,
                         cos0_ref, sin0_ref, cos1_ref, sin1_ref, cos2_ref, sin2_ref,
                         qkv0_ref, qkv1_ref, qkv2_ref, sz_ref, gates_ref, hs_ref):
    ts = x_ref.shape[0] // IN_SUB
    for sub in range(IN_SUB):
        rows = pl.ds(sub * ts, ts)
        hn = _rms_normalize(x_ref[rows, :]) * nm_ref[...]
        _attn_natural(hn.astype(MM_DTYPE), rows, wq0_ref, bq0_ref, wz_ref, bz_ref, wgt_ref,
                      bgt_ref, gq_ref, gk_ref, cos0_ref, sin0_ref, qkv0_ref, sz_ref, gates_ref)
        _attn_dilated(hn, sub, w1_ref, b1_ref, w2_ref, b2_ref, gq_ref, gk_ref, cos1_ref,
                      sin1_ref, cos2_ref, sin2_ref, qkv1_ref, qkv2_ref, hs_ref)


def _in_proj_attn(x, nm, attn_w, tabs):
    B, S, _ = x.shape
    tm = TM_IN
    d1, d2 = ATTN_PATTERNS[1][1], ATTN_PATTERNS[2][1]
    (cos0, sin0), (cos1, sin1), (cos2, sin2) = tabs
    row = lambda w: pl.BlockSpec((None, tm, w), lambda b, i: (b, i, 0))
    tab_specs = [
        pl.BlockSpec((tm, HEAD_DIM), lambda b, i: (i, 0)),
        pl.BlockSpec((tm, HEAD_DIM), lambda b, i: (i, 0)),
        pl.BlockSpec((d1, tm // d1, HEAD_DIM), lambda b, i: (0, i, 0)),
        pl.BlockSpec((d1, tm // d1, HEAD_DIM), lambda b, i: (0, i, 0)),
        pl.BlockSpec((d2, tm // d2, HEAD_DIM), lambda b, i: (0, i, 0)),
        pl.BlockSpec((d2, tm // d2, HEAD_DIM), lambda b, i: (0, i, 0)),
    ]
    in_specs = ([row(D_MODEL), _const_spec(nm.shape)]
                + [_const_spec(a.shape) for a in attn_w] + tab_specs)
    out_shape = (
        jax.ShapeDtypeStruct((B, S, GROUP_W), MM_DTYPE),
        jax.ShapeDtypeStruct((B, d1, S // d1, GROUP_W), MM_DTYPE),
        jax.ShapeDtypeStruct((B, d2, S // d2, GROUP_W), MM_DTYPE),
        jax.ShapeDtypeStruct((B, S, ATT_W), MM_DTYPE),
        jax.ShapeDtypeStruct((B, S, 2 * D_MODEL), MM_DTYPE),
    )
    out_specs = (
        row(GROUP_W),
        pl.BlockSpec((None, d1, tm // d1, GROUP_W), lambda b, i: (b, 0, i, 0)),
        pl.BlockSpec((None, d2, tm // d2, GROUP_W), lambda b, i: (b, 0, i, 0)),
        row(ATT_W),
        row(2 * D_MODEL),
    )
    return pl.pallas_call(
        _in_proj_attn_kernel,
        grid=(B, S // tm),
        in_specs=in_specs,
        out_specs=out_specs,
        out_shape=out_shape,
        scratch_shapes=[pltpu.VMEM((IN_SUB, D_MODEL // LANES, tm // IN_SUB, LANES),
                                   jnp.float32)],
        compiler_params=_params(2),
        name="in_proj_attn",
    )(x, nm, *attn_w, cos0, sin0, cos1, sin1, cos2, sin2)


def _rnn_kernel(x_ref, nm_ref, wx_ref, bx_ref, wz_ref, bz_ref, cw_ref, cb_ref, wg_ref, bg_ref,
                lam_ref, wo_ref, yr_ref, hs_ref, xs_ref, h_ref, ys_ref):
    nb, tt, _ = x_ref.shape
    ts = RNN_SUB_T
    n_sub = tt // ts
    rows = nb * ts
    n_slabs = D_MODEL // LANES
    hist = (CONV_W - 1) * nb
    step = pl.program_id(0)

    @pl.when(step == 0)
    def _():
        xs_ref[0:hist, :] = jnp.zeros((hist, D_RNN), jnp.float32)
        h_ref[...] = jnp.zeros_like(h_ref)

    staged = []
    for s in range(n_sub):
        for b in range(nb):
            hn = _rms_normalize(x_ref[b, s * ts:(s + 1) * ts, :]) * nm_ref[...]
            for j in range(n_slabs):
                hs_ref[s, j, pl.ds(b * RNN_PITCH, ts), :] = hn[:, j * LANES:(j + 1) * LANES]
        hp = jnp.concatenate(
            [jnp.concatenate([hs_ref[s, j, pl.ds(t, nb, stride=RNN_PITCH), :]
                              for j in range(n_slabs)], axis=1) for t in range(ts)],
            axis=0).astype(MM_DTYPE)
        xr = _dot(hp, wx_ref[...]) + bx_ref[...]
        z = _dot(hp, wz_ref[...]) + bz_ref[...]
        xs_ref[pl.ds(hist + s * rows, rows), :] = xr
        xc = cb_ref[...]
        for k in range(CONV_W):
            xc = xc + cw_ref[k:k + 1, :] * xs_ref[pl.ds(s * rows + k * nb, rows), :]
        xcb = xc.astype(MM_DTYPE)
        pre = [_dot(xcb[:, n * RNN_BLOCK_W:(n + 1) * RNN_BLOCK_W], wg_ref[n])
               for n in range(RNN_BLOCKS)]
        staged.append((xc, pre, z))

    neg_lam = -lam_ref[...]
    softplus = jnp.maximum(neg_lam, 0.0) + jnp.log(1.0 + jnp.exp(-jnp.abs(neg_lam)))
    log_a_scale = (-LRU_C) * softplus
    h = h_ref[...]
    for s in range(n_sub):
        xc, pre, z = staged[s]
        r = _sigmoid(jnp.concatenate([t[:, :RNN_BLOCK_W] for t in pre], axis=1)
                     + bg_ref[0:1, :])
        gi = _sigmoid(jnp.concatenate([t[:, RNN_BLOCK_W:] for t in pre], axis=1)
                      + bg_ref[1:2, :])
        a = jnp.exp(log_a_scale * r)
        y1 = 1.0 - a * a
        mult = jnp.where(y1 > 0.0, y1 * lax.rsqrt(y1), 0.0)
        if s == 0:
            row_id = lax.broadcasted_iota(jnp.int32, (rows, 1), 0) + step * (nb * tt)
            mult = jnp.where(row_id < nb, 1.0, mult)
        u = mult * (gi * xc)

        hs = []
        for t in range(ts):
            h = a[t * nb:(t + 1) * nb] * h + u[t * nb:(t + 1) * nb]
            hs.append(h)
        y = (jnp.concatenate(hs, axis=0) * (z * _sigmoid(z))).astype(MM_DTYPE)
        yr = _dot(y, wo_ref[...])

        for j in range(n_slabs):
            for t in range(ts):
                ys_ref[s, j, pl.ds(t * RNN_PITCH, nb), :] = yr[t * nb:(t + 1) * nb,
                                                               j * LANES:(j + 1) * LANES]
        for b in range(nb):
            yr_ref[b, s * ts:(s + 1) * ts, :] = jnp.concatenate(
                [ys_ref[s, j, pl.ds(b, ts, stride=RNN_PITCH), :] for j in range(n_slabs)],
                axis=1).astype(yr_ref.dtype)
    h_ref[...] = h
    xs_ref[0:hist, :] = xs_ref[pl.ds(n_sub * rows, hist), :]


def _rnn_branch(x, nm, rnn_w):
    B, S, _ = x.shape
    tt = TT_RNN
    n_slabs = D_MODEL // LANES
    return pl.pallas_call(
        _rnn_kernel,
        grid=(S // tt,),
        in_specs=([pl.BlockSpec((B, tt, D_MODEL), lambda i: (0, i, 0)), _const_spec(nm.shape)]
                  + [_const_spec(a.shape) for a in rnn_w]),
        out_specs=pl.BlockSpec((B, tt, D_MODEL), lambda i: (0, i, 0)),
        out_shape=jax.ShapeDtypeStruct((B, S, D_MODEL), MM_DTYPE),
        scratch_shapes=[
            pltpu.VMEM((tt // RNN_SUB_T, n_slabs, B * RNN_PITCH, LANES), jnp.float32),
            pltpu.VMEM(((CONV_W - 1) * B + B * tt, D_RNN), jnp.float32),
            pltpu.VMEM((B, D_RNN), jnp.float32),
            pltpu.VMEM((tt // RNN_SUB_T, n_slabs, RNN_SUB_T * RNN_PITCH, LANES), jnp.float32),
        ],
        compiler_params=_params(1),
        name="rnn_branch",
    )(x, nm, *rnn_w)


def _attn_kernel(q0, k0, v0, q1, k1, v1, q2, k2, v2, sz_ref, y_ref, o_nat, l_nat):
    S = sz_ref.shape[0]
    n_blk = S // ATT_BLK
    blk3 = (n_blk, ATT_BLK, HEAD_DIM)
    qi = lax.broadcasted_iota(jnp.int32, (1, ATT_BLK, ATT_BLK), 1)
    kj = lax.broadcasted_iota(jnp.int32, (1, ATT_BLK, ATT_BLK), 2)
    cur_ok = kj <= qi
    blk_id = lax.broadcasted_iota(jnp.int32, (n_blk, 1, 1), 0)
    qk_dims = (((2,), (2,)), ((0,), (0,)))
    pv_dims = (((2,), (1,)), ((0,), (0,)))

    for g, (q_ref, k_ref, v_ref) in enumerate(((q0, k0, v0), (q1, k1, v1), (q2, k2, v2))):
        dil = ATTN_PATTERNS[g][1]
        blocks_per_class = n_blk // dil
        q = q_ref[...].reshape(blk3)
        k = k_ref[...].reshape(blk3)
        v = v_ref[...].reshape(blk3)
        v1 = jnp.concatenate([v, jnp.ones_like(v)], axis=2)
        if blocks_per_class > 1:
            shift = lambda t: jnp.concatenate([t[:1], t[:-1]], axis=0)
            kk = jnp.concatenate([shift(k), k], axis=1)
            vv = jnp.concatenate([shift(v1), v1], axis=1)
            s = lax.dot_general(q, kk, qk_dims, preferred_element_type=jnp.float32)
            first = (blk_id & (blocks_per_class - 1)) == 0
            prev_ok = kj >= qi + jnp.where(first, ATT_BLK, 0)
            s_prev = jnp.where(prev_ok, s[:, :, :ATT_BLK], NEG_INF)
            s_cur = jnp.where(cur_ok, s[:, :, ATT_BLK:], NEG_INF)
            m = jnp.max(jnp.maximum(s_prev, s_cur), axis=-1, keepdims=True)
            e = jnp.concatenate([jnp.exp(s_prev - m), jnp.exp(s_cur - m)], axis=2)
        else:
            vv = v1
            s = lax.dot_general(q, k, qk_dims, preferred_element_type=jnp.float32)
            s = jnp.where(cur_ok, s, NEG_INF)
            m = jnp.max(s, axis=-1, keepdims=True)
            e = jnp.exp(s - m)
        od = lax.dot_general(e.astype(MM_DTYPE), vv, pv_dims,
                             preferred_element_type=jnp.float32)
        den = od[:, :, HEAD_DIM:]
        o = od[:, :, :HEAD_DIM] * (1.0 / den)
        lse = m + jnp.log(den)
        if dil == 1:
            o_nat[g] = o.reshape(S, HEAD_DIM)
            l_nat[g] = lse.reshape(S, HEAD_DIM)
        else:
            for nb in range(n_blk):
                c, m0 = divmod(nb, blocks_per_class)
                idx = pl.ds(m0 * ATT_BLK * dil + c, ATT_BLK, stride=dil)
                o_nat[g, idx, :] = o[nb]
                l_nat[g, idx, :] = lse[nb]

    def merge(i, carry):
        rows = pl.ds(pl.multiple_of(i * COMBINE_ROWS, COMBINE_ROWS), COMBINE_ROWS)
        ls = [l_nat[g, rows, :] for g in range(N_GROUPS)]
        m = functools.reduce(jnp.maximum, ls)
        ws = [jnp.exp(l - m) for l in ls]
        den = functools.reduce(lambda p, q: p + q, ws)
        att = functools.reduce(
            lambda p, q: p + q, [w * o_nat[g, rows, :] for g, w in enumerate(ws)])
        att = att * (1.0 / den)
        y_ref[rows, :] = (att * sz_ref[rows, :].astype(jnp.float32)).astype(y_ref.dtype)
        return carry
    lax.fori_loop(0, S // COMBINE_ROWS, merge, 0)


def _attention(qkv0, qkv1, qkv2, sz):
    B, S, _ = qkv0.shape
    qkv1 = qkv1.reshape(B, S, GROUP_W)
    qkv2 = qkv2.reshape(B, S, GROUP_W)
    in_specs = []
    for _ in range(N_GROUPS):
        for part in range(3):
            in_specs.append(pl.BlockSpec((None, S, HEAD_DIM),
                                         lambda b, h, part=part: (b, 0, part * HEADS + h)))
    in_specs.append(pl.BlockSpec((None, S, HEAD_DIM), lambda b, h: (b, 0, h)))
    return pl.pallas_call(
        _attn_kernel,
        grid=(B, HEADS),
        in_specs=in_specs,
        out_specs=pl.BlockSpec((None, S, HEAD_DIM), lambda b, h: (b, 0, h)),
        out_shape=jax.ShapeDtypeStruct((B, S, ATT_W), MM_DTYPE),
        scratch_shapes=[
            pltpu.VMEM((N_GROUPS, S, HEAD_DIM), jnp.float32),
            pltpu.VMEM((N_GROUPS, S, HEAD_DIM), jnp.float32),
        ],
        compiler_params=_params(2),
        name="attention",
    )(qkv0, qkv0, qkv0, qkv1, qkv1, qkv1, qkv2, qkv2, qkv2, sz)


def _out_kernel(x_ref, p_ref, ya_ref, yr_ref, g_ref, woa_ref, wout_ref, np_ref, wpg_ref,
                bpg_ref, wple_ref, o_ref):
    tm = x_ref.shape[0]
    halves = [pl.ds(i * (tm // OUT_SUB), tm // OUT_SUB) for i in range(OUT_SUB)]
    staged = []
    for rows in halves:
        pe = _dot(p_ref[rows, :].astype(MM_DTYPE), wple_ref[...])
        ya = _dot(ya_ref[rows, :], woa_ref[...])
        g0 = g_ref[rows, :D_MODEL].astype(jnp.float32)
        g1 = g_ref[rows, D_MODEL:].astype(jnp.float32)
        merged = g0 * yr_ref[rows, :].astype(jnp.float32) + g1 * ya
        x2 = x_ref[rows, :] + _dot(merged.astype(MM_DTYPE), wout_ref[...])
        staged.append((x2, pe))
    for rows, (x2, pe) in zip(halves, staged):
        n2 = (_rms_normalize(x2) * np_ref[...]).astype(MM_DTYPE)
        pg = _sigmoid(_dot(n2, wpg_ref[...]) + bpg_ref[...])
        o_ref[rows, :] = x2 + pg * pe


def _out_proj(x, p, ya, yr, gates, woa, wout, npl, wpg, bpg, wple):
    B, S, _ = x.shape
    tm = TM_OUT
    row = lambda w: pl.BlockSpec((None, tm, w), lambda b, i: (b, i, 0))
    return pl.pallas_call(
        _out_kernel,
        grid=(B, S // tm),
        in_specs=[
            row(D_MODEL), row(PLE_DIM), row(ATT_W), row(D_MODEL), row(2 * D_MODEL),
            _const_spec(woa.shape), _const_spec(wout.shape), _const_spec(npl.shape),
            _const_spec(wpg.shape), _const_spec(bpg.shape), _const_spec(wple.shape),
        ],
        out_specs=row(D_MODEL),
        out_shape=jax.ShapeDtypeStruct((B, S, D_MODEL), x.dtype),
        compiler_params=_params(2),
        name="out_proj",
    )(x, p, ya, yr, gates, woa, wout, npl, wpg, bpg, wple)


def _rope_tables(s):
    pos = jnp.arange(s, dtype=jnp.float32)
    inv_freq = ROPE_THETA ** (-jnp.arange(0, HEAD_DIM, 2, dtype=jnp.float32) / HEAD_DIM)
    ang = pos[:, None] * inv_freq[None, :]
    cos, sin = jnp.cos(ang), jnp.sin(ang)
    cos2 = jnp.concatenate([cos, cos], axis=1)
    sin2 = jnp.concatenate([-sin, sin], axis=1)
    tabs = []
    for _, dil in ATTN_PATTERNS:
        if dil == 1:
            tabs.append((cos2, sin2))
        else:
            perm = lambda t: t.reshape(s // dil, dil, HEAD_DIM).transpose(1, 0, 2)
            tabs.append((perm(cos2), perm(sin2)))
    return tabs


def kernel(x, p, norm_mix, w_in, b_in, conv_w, conv_b, w_rg_a, b_rg_a, w_rg_x, b_rg_x,
           lru_lambda, q_norm, k_norm, w_o_rnn, w_o_att, w_out, norm_ple, w_ple_gate,
           b_ple_gate, w_ple):
    depth = w_in.shape[0]
    s = x.shape[1]
    tabs = _rope_tables(s)
    f32 = jnp.float32
    for layer in range(depth):
        w = w_in[layer]
        b = b_in[layer].astype(f32)[None, :]
        nm = norm_mix[layer].astype(f32)[None, :]
        cols = lambda a, lo, hi: a[:, lo:hi]
        grp = lambda a, g: cols(a, OFF_QKV + g * GROUP_W, OFF_QKV + (g + 1) * GROUP_W)
        wb = lambda lo, hi: (cols(w, lo, hi).astype(MM_DTYPE), cols(b, lo, hi))
        gq = q_norm[layer].astype(f32) * (HEAD_DIM ** -0.5)
        gk = k_norm[layer].astype(f32)
        attn_w = (grp(w, 0).astype(MM_DTYPE), grp(b, 0),
                  *wb(OFF_Z_ATT, OFF_GATES), *wb(OFF_GATES, w.shape[1]),
                  grp(w, 1).astype(MM_DTYPE), grp(b, 1),
                  grp(w, 2).astype(MM_DTYPE), grp(b, 2), gq, gk)
        wg = jnp.concatenate([w_rg_a[layer], w_rg_x[layer]], axis=2).astype(MM_DTYPE)
        bg = jnp.stack([b_rg_a[layer], b_rg_x[layer]], axis=0).astype(f32)
        rnn_w = (*wb(0, OFF_Z_RNN), *wb(OFF_Z_RNN, OFF_QKV), conv_w[layer].astype(f32),
                 conv_b[layer].astype(f32)[None, :], wg, bg,
                 lru_lambda[layer].astype(f32)[None, :], w_o_rnn[layer].astype(MM_DTYPE))
        qkv0, qkv1, qkv2, sz, gates = _in_proj_attn(x, nm, attn_w, tabs)
        yr = _rnn_branch(x, nm, rnn_w)
        ya = _attention(qkv0, qkv1, qkv2, sz)
        x = _out_proj(
            x, p[layer], ya, yr, gates,
            w_o_att[layer].astype(MM_DTYPE), w_out[layer].astype(MM_DTYPE),
            norm_ple[layer].astype(f32)[None, :], w_ple_gate[layer].astype(MM_DTYPE),
            b_ple_gate[layer].astype(f32)[None, :], w_ple[layer].astype(MM_DTYPE))
    return x
```

```python
import functools
import math

import jax
import jax.numpy as jnp
import numpy as np
from jax import lax
from jax.experimental import pallas as pl
from jax.experimental.pallas import tpu as pltpu

D_MODEL = 1024
PLE_DIM = 256
D_RNN = 1280
RNN_BLOCKS = 10
RNN_BLOCK_W = D_RNN // RNN_BLOCKS
CONV_W = 4
LRU_C = 8.0
HEAD_DIM = 128
HEADS = 4
ATTN_PATTERNS = ((128, 1), (512, 4), (2048, 16))
N_GROUPS = len(ATTN_PATTERNS)
ATT_W = HEADS * HEAD_DIM
GROUP_W = 3 * ATT_W
ATT_BLK = 128
ROPE_THETA = 10000.0
EPS = 1e-6

OFF_Z_RNN = D_RNN
OFF_QKV = 2 * D_RNN
OFF_Z_ATT = OFF_QKV + N_GROUPS * GROUP_W
OFF_GATES = OFF_Z_ATT + ATT_W

LANES = 128
SUBLANES = 8
VMEM_LIMIT_BYTES = 56 * 1024 * 1024

TM_IN = 512
IN_SUB = 2
TT_RNN = 64
RNN_SUB_T = 16
RNN_PITCH = 24
TM_OUT = 1024
OUT_SUB = 2
COMBINE_ROWS = 256

MM_DTYPE = jnp.bfloat16
LOG2_E = math.log2(math.e)
LN_2 = math.log(2.0)
NEG_INF = float("-inf")


def _sigmoid(v):
    return 1.0 / (1.0 + jnp.exp2(v * (-LOG2_E)))


def _rms_normalize(v):
    var = jnp.mean(v * v, axis=-1, keepdims=True)
    return v * lax.rsqrt(var + EPS)


def _dot(a, b):
    return jnp.dot(a, b, preferred_element_type=jnp.float32)


def _const_spec(shape):
    zeros = (0,) * len(shape)
    return pl.BlockSpec(shape, lambda *_: zeros, pipeline_mode=pl.Buffered(1))


def _params(n_grid):
    return pltpu.CompilerParams(
        dimension_semantics=("arbitrary",) * n_grid,
        vmem_limit_bytes=VMEM_LIMIT_BYTES,
    )


def _qk_epilogue(acc, gq, gk, cos2, sin2):
    outs = []
    for part, gain in ((0, gq), (1, gk)):
        for h in range(HEADS):
            lo = part * ATT_W + h * HEAD_DIM
            t = _rms_normalize(acc[:, lo:lo + HEAD_DIM]) * gain
            outs.append(t * cos2 + pltpu.roll(t, HEAD_DIM // 2, 1) * sin2)
    outs.append(acc[:, 2 * ATT_W:])
    return jnp.concatenate(outs, axis=1)


def _attn_natural(hb, rows, wq0_ref, bq0_ref, wz_ref, bz_ref, wgt_ref, bgt_ref, gq_ref, gk_ref,
                  cos0_ref, sin0_ref, qkv0_ref, sz_ref, gates_ref):
    acc = _dot(hb, wq0_ref[...]) + bq0_ref[...]
    qkv0_ref[rows, :] = _qk_epilogue(acc, gq_ref[0:1], gk_ref[0:1], cos0_ref[rows, :],
                                     sin0_ref[rows, :]).astype(qkv0_ref.dtype)
    z = _dot(hb, wz_ref[...]) + bz_ref[...]
    sz_ref[rows, :] = (z * _sigmoid(z)).astype(sz_ref.dtype)
    g = _dot(hb, wgt_ref[...]) + bgt_ref[...]
    gates_ref[rows, :] = _sigmoid(g).astype(gates_ref.dtype)


def _attn_dilated(hn, sub, w1_ref, b1_ref, w2_ref, b2_ref, gq_ref, gk_ref, cos1_ref, sin1_ref,
                  cos2_ref, sin2_ref, qkv1_ref, qkv2_ref, hs_ref):
    tm = hn.shape[0]
    n_slabs = D_MODEL // LANES
    for j in range(n_slabs):
        hs_ref[sub, j] = hn[:, j * LANES:(j + 1) * LANES]
    for (dil, w_ref, b_ref, gi, cos_ref, sin_ref, out_ref) in (
            (ATTN_PATTERNS[1][1], w1_ref, b1_ref, 1, cos1_ref, sin1_ref, qkv1_ref),
            (ATTN_PATTERNS[2][1], w2_ref, b2_ref, 2, cos2_ref, sin2_ref, qkv2_ref)):
        per = tm // dil
        classes = []
        for c in range(dil):
            classes.append(jnp.concatenate(
                [hs_ref[sub, j, pl.ds(c, per, stride=dil), :] for j in range(n_slabs)],
                axis=1))
        hp = jnp.concatenate(classes, axis=0).astype(MM_DTYPE)
        acc = _dot(hp, w_ref[...]) + b_ref[...]
        part = pl.ds(sub * per, per)
        cos2 = cos_ref[:, part, :].reshape(tm, HEAD_DIM)
        sin2 = sin_ref[:, part, :].reshape(tm, HEAD_DIM)
        res = _qk_epilogue(acc, gq_ref[gi:gi + 1], gk_ref[gi:gi + 1], cos2, sin2)
        out_ref[:, part, :] = res.astype(out_ref.dtype).reshape(dil, per, GROUP_W)


def _in_proj_attn_kernel(x_ref, nm_ref,
                         wq0_ref, bq0_ref, wz_ref, bz_ref, wgt_ref, bgt_ref, w1_ref, b1_ref,
                         w2_ref, b2_ref, gq_ref, gk_ref,
                         cos0_ref, sin0_ref, cos1_ref, sin1_ref, cos2_ref, sin2_ref,
                         qkv0_ref, qkv1_ref, qkv2_ref, sz_ref, gates_ref, hs_ref):
    ts = x_ref.shape[0] // IN_SUB
    for sub in range(IN_SUB):
        rows = pl.ds(sub * ts, ts)
        hn = _rms_normalize(x_ref[rows, :]) * nm_ref[...]
        _attn_natural(hn.astype(MM_DTYPE), rows, wq0_ref, bq0_ref, wz_ref, bz_ref, wgt_ref,
                      bgt_ref, gq_ref, gk_ref, cos0_ref, sin0_ref, qkv0_ref, sz_ref, gates_ref)
        _attn_dilated(hn, sub, w1_ref, b1_ref, w2_ref, b2_ref, gq_ref, gk_ref, cos1_ref,
                      sin1_ref, cos2_ref, sin2_ref, qkv1_ref, qkv2_ref, hs_ref)


def _in_proj_attn(x, nm, attn_w, tabs):
    B, S, _ = x.shape
    tm = TM_IN
    d1, d2 = ATTN_PATTERNS[1][1], ATTN_PATTERNS[2][1]
    (cos0, sin0), (cos1, sin1), (cos2, sin2) = tabs
    row = lambda w: pl.BlockSpec((None, tm, w), lambda b, i: (b, i, 0))
    tab_specs = [
        pl.BlockSpec((tm, HEAD_DIM), lambda b, i: (i, 0)),
        pl.BlockSpec((tm, HEAD_DIM), lambda b, i: (i, 0)),
        pl.BlockSpec((d1, tm // d1, HEAD_DIM), lambda b, i: (0, i, 0)),
        pl.BlockSpec((d1, tm // d1, HEAD_DIM), lambda b, i: (0, i, 0)),
        pl.BlockSpec((d2, tm // d2, HEAD_DIM), lambda b, i: (0, i, 0)),
        pl.BlockSpec((d2, tm // d2, HEAD_DIM), lambda b, i: (0, i, 0)),
    ]
    in_specs = ([row(D_MODEL), _const_spec(nm.shape)]
                + [_const_spec(a.shape) for a in attn_w] + tab_specs)
    out_shape = (
        jax.ShapeDtypeStruct((B, S, GROUP_W), MM_DTYPE),
        jax.ShapeDtypeStruct((B, d1, S // d1, GROUP_W), MM_DTYPE),
        jax.ShapeDtypeStruct((B, d2, S // d2, GROUP_W), MM_DTYPE),
        jax.ShapeDtypeStruct((B, S, ATT_W), MM_DTYPE),
        jax.ShapeDtypeStruct((B, S, 2 * D_MODEL), MM_DTYPE),
    )
    out_specs = (
        row(GROUP_W),
        pl.BlockSpec((None, d1, tm // d1, GROUP_W), lambda b, i: (b, 0, i, 0)),
        pl.BlockSpec((None, d2, tm // d2, GROUP_W), lambda b, i: (b, 0, i, 0)),
        row(ATT_W),
        row(2 * D_MODEL),
    )
    return pl.pallas_call(
        _in_proj_attn_kernel,
        grid=(B, S // tm),
        in_specs=in_specs,
        out_specs=out_specs,
        out_shape=out_shape,
        scratch_shapes=[pltpu.VMEM((IN_SUB, D_MODEL // LANES, tm // IN_SUB, LANES),
                                   jnp.float32)],
        compiler_params=_params(2),
        name="in_proj_attn",
    )(x, nm, *attn_w, cos0, sin0, cos1, sin1, cos2, sin2)


def _rnn_kernel(x_ref, nm_ref, wx_ref, bx_ref, wz_ref, bz_ref, cw_ref, cb_ref, wg_ref, bg_ref,
                lam_ref, wo_ref, yr_ref, hs_ref, xs_ref, h_ref, ys_ref):
    nb, tt, _ = x_ref.shape
    ts = RNN_SUB_T
    n_sub = tt // ts
    rows = nb * ts
    n_slabs = D_MODEL // LANES
    hist = (CONV_W - 1) * nb
    step = pl.program_id(0)

    @pl.when(step == 0)
    def _():
        xs_ref[0:hist, :] = jnp.zeros((hist, D_RNN), jnp.float32)
        h_ref[...] = jnp.zeros_like(h_ref)

    neg_lam = -lam_ref[...]
    softplus = jnp.maximum(neg_lam, 0.0) + jnp.log(1.0 + jnp.exp(-jnp.abs(neg_lam)))
    log2_a_scale = (-LRU_C * LOG2_E) * softplus

    def front(s):
        for b in range(nb):
            hn = _rms_normalize(x_ref[b, s * ts:(s + 1) * ts, :]) * nm_ref[...]
            for j in range(n_slabs):
                hs_ref[s, j, pl.ds(b * RNN_PITCH, ts), :] = hn[:, j * LANES:(j + 1) * LANES]
        hp = jnp.concatenate(
            [jnp.concatenate([hs_ref[s, j, pl.ds(t, nb, stride=RNN_PITCH), :]
                              for j in range(n_slabs)], axis=1) for t in range(ts)],
            axis=0).astype(MM_DTYPE)
        xr = _dot(hp, wx_ref[...]) + bx_ref[...]
        z = _dot(hp, wz_ref[...]) + bz_ref[...]
        xs_ref[pl.ds(hist + s * rows, rows), :] = xr
        xc = cb_ref[...]
        for k in range(CONV_W):
            xc = xc + cw_ref[k:k + 1, :] * xs_ref[pl.ds(s * rows + k * nb, rows), :]
        xcb = xc.astype(MM_DTYPE)
        pre = [_dot(xcb[:, n * RNN_BLOCK_W:(n + 1) * RNN_BLOCK_W], wg_ref[n])
               for n in range(RNN_BLOCKS)]
        return xc, pre, z

    def back(s, staged, h):
        xc, pre, z = staged
        r = _sigmoid(jnp.concatenate([t[:, :RNN_BLOCK_W] for t in pre], axis=1)
                     + bg_ref[0:1, :])
        gi = _sigmoid(jnp.concatenate([t[:, RNN_BLOCK_W:] for t in pre], axis=1)
                      + bg_ref[1:2, :])
        a = jnp.exp2(log2_a_scale * r)
        y1 = 1.0 - a * a
        mult = jnp.where(y1 > 0.0, y1 * lax.rsqrt(y1), 0.0)
        gx = gi * xc
        hs = []
        for t in range(ts):
            sl = slice(t * nb, (t + 1) * nb)
            m_t = mult[sl]
            if s == 0 and t == 0:
                m_t = jnp.where(step == 0, 1.0, m_t)
            h = a[sl] * h + m_t * gx[sl]
            hs.append(h)
        y = (jnp.concatenate(hs, axis=0) * (z * _sigmoid(z))).astype(MM_DTYPE)
        yr = _dot(y, wo_ref[...])
        for j in range(n_slabs):
            for t in range(ts):
                ys_ref[s, j, pl.ds(t * RNN_PITCH, nb), :] = yr[t * nb:(t + 1) * nb,
                                                               j * LANES:(j + 1) * LANES]
        for b in range(nb):
            yr_ref[b, s * ts:(s + 1) * ts, :] = jnp.concatenate(
                [ys_ref[s, j, pl.ds(b, ts, stride=RNN_PITCH), :] for j in range(n_slabs)],
                axis=1).astype(yr_ref.dtype)
        return h

    h = h_ref[...]
    staged = {0: front(0)}
    for s in range(n_sub):
        if s + 1 < n_sub:
            staged[s + 1] = front(s + 1)
        h = back(s, staged.pop(s), h)
    h_ref[...] = h
    xs_ref[0:hist, :] = xs_ref[pl.ds(n_sub * rows, hist), :]


def _rnn_branch(x, nm, rnn_w):
    B, S, _ = x.shape
    tt = TT_RNN
    n_slabs = D_MODEL // LANES
    return pl.pallas_call(
        _rnn_kernel,
        grid=(S // tt,),
        in_specs=([pl.BlockSpec((B, tt, D_MODEL), lambda i: (0, i, 0)), _const_spec(nm.shape)]
                  + [_const_spec(a.shape) for a in rnn_w]),
        out_specs=pl.BlockSpec((B, tt, D_MODEL), lambda i: (0, i, 0)),
        out_shape=jax.ShapeDtypeStruct((B, S, D_MODEL), MM_DTYPE),
        scratch_shapes=[
            pltpu.VMEM((tt // RNN_SUB_T, n_slabs, B * RNN_PITCH, LANES), jnp.float32),
            pltpu.VMEM(((CONV_W - 1) * B + B * tt, D_RNN), jnp.float32),
            pltpu.VMEM((B, D_RNN), jnp.float32),
            pltpu.VMEM((tt // RNN_SUB_T, n_slabs, RNN_SUB_T * RNN_PITCH, LANES), jnp.float32),
        ],
        compiler_params=_params(1),
        name="rnn_branch",
    )(x, nm, *rnn_w)


def _attn_kernel(q0, k0, v0, q1, k1, v1, q2, k2, v2, sz_ref, y_ref, o_nat, l_nat):
    S = sz_ref.shape[0]
    n_blk = S // ATT_BLK
    blk3 = (n_blk, ATT_BLK, HEAD_DIM)
    qi = lax.broadcasted_iota(jnp.int32, (1, ATT_BLK, ATT_BLK), 1)
    kj = lax.broadcasted_iota(jnp.int32, (1, ATT_BLK, ATT_BLK), 2)
    cur_ok = kj <= qi
    blk_id = lax.broadcasted_iota(jnp.int32, (n_blk, 1, 1), 0)
    qk_dims = (((2,), (2,)), ((0,), (0,)))
    pv_dims = (((2,), (1,)), ((0,), (0,)))

    for g, (q_ref, k_ref, v_ref) in enumerate(((q0, k0, v0), (q1, k1, v1), (q2, k2, v2))):
        dil = ATTN_PATTERNS[g][1]
        blocks_per_class = n_blk // dil
        q = q_ref[...].reshape(blk3)
        k = k_ref[...].reshape(blk3)
        v = v_ref[...].reshape(blk3)
        v1 = jnp.concatenate([v, jnp.ones_like(v)], axis=2)
        if blocks_per_class > 1:
            shift = lambda t: jnp.concatenate([t[:1], t[:-1]], axis=0)
            kk = jnp.concatenate([shift(k), k], axis=1)
            vv = jnp.concatenate([shift(v1), v1], axis=1)
            s = lax.dot_general(q, kk, qk_dims, preferred_element_type=jnp.float32)
            first = (blk_id & (blocks_per_class - 1)) == 0
            prev_ok = kj >= qi + jnp.where(first, ATT_BLK, 0)
            s_prev = jnp.where(prev_ok, s[:, :, :ATT_BLK], NEG_INF)
            s_cur = jnp.where(cur_ok, s[:, :, ATT_BLK:], NEG_INF)
            m = jnp.max(jnp.maximum(s_prev, s_cur), axis=-1, keepdims=True)
            e = jnp.concatenate([jnp.exp2(s_prev - m), jnp.exp2(s_cur - m)], axis=2)
        else:
            vv = v1
            s = lax.dot_general(q, k, qk_dims, preferred_element_type=jnp.float32)
            s = jnp.where(cur_ok, s, NEG_INF)
            m = jnp.max(s, axis=-1, keepdims=True)
            e = jnp.exp2(s - m)
        od = lax.dot_general(e.astype(MM_DTYPE), vv, pv_dims,
                             preferred_element_type=jnp.float32)
        den = od[:, :, HEAD_DIM:]
        o = od[:, :, :HEAD_DIM] * (1.0 / den)
        lse = m * LN_2 + jnp.log(den)
        if dil == 1:
            o_nat[g] = o.reshape(S, HEAD_DIM)
            l_nat[g] = lse.reshape(S, HEAD_DIM)
        else:
            for nb in range(n_blk):
                c, m0 = divmod(nb, blocks_per_class)
                idx = pl.ds(m0 * ATT_BLK * dil + c, ATT_BLK, stride=dil)
                o_nat[g, idx, :] = o[nb]
                l_nat[g, idx, :] = lse[nb]

    def merge(i, carry):
        rows = pl.ds(pl.multiple_of(i * COMBINE_ROWS, COMBINE_ROWS), COMBINE_ROWS)
        ls = [l_nat[g, rows, :] for g in range(N_GROUPS)]
        m = functools.reduce(jnp.maximum, ls)
        ws = [jnp.exp(l - m) for l in ls]
        den = functools.reduce(lambda p, q: p + q, ws)
        att = functools.reduce(
            lambda p, q: p + q, [w * o_nat[g, rows, :] for g, w in enumerate(ws)])
        att = att * (1.0 / den)
        y_ref[rows, :] = (att * sz_ref[rows, :].astype(jnp.float32)).astype(y_ref.dtype)
        return carry
    lax.fori_loop(0, S // COMBINE_ROWS, merge, 0)


def _attention(qkv0, qkv1, qkv2, sz):
    B, S, _ = qkv0.shape
    qkv1 = qkv1.reshape(B, S, GROUP_W)
    qkv2 = qkv2.reshape(B, S, GROUP_W)
    in_specs = []
    for _ in range(N_GROUPS):
        for part in range(3):
            in_specs.append(pl.BlockSpec((None, S, HEAD_DIM),
                                         lambda b, h, part=part: (b, 0, part * HEADS + h)))
    in_specs.append(pl.BlockSpec((None, S, HEAD_DIM), lambda b, h: (b, 0, h)))
    return pl.pallas_call(
        _attn_kernel,
        grid=(B, HEADS),
        in_specs=in_specs,
        out_specs=pl.BlockSpec((None, S, HEAD_DIM), lambda b, h: (b, 0, h)),
        out_shape=jax.ShapeDtypeStruct((B, S, ATT_W), MM_DTYPE),
        scratch_shapes=[
            pltpu.VMEM((N_GROUPS, S, HEAD_DIM), jnp.float32),
            pltpu.VMEM((N_GROUPS, S, HEAD_DIM), jnp.float32),
        ],
        compiler_params=_params(2),
        name="attention",
    )(qkv0, qkv0, qkv0, qkv1, qkv1, qkv1, qkv2, qkv2, qkv2, sz)


def _out_kernel(x_ref, p_ref, ya_ref, yr_ref, g_ref, woa_ref, wout_ref, np_ref, wpg_ref,
                bpg_ref, wple_ref, o_ref):
    tm = x_ref.shape[0]
    halves = [pl.ds(i * (tm // OUT_SUB), tm // OUT_SUB) for i in range(OUT_SUB)]
    staged = []
    for rows in halves:
        pe = _dot(p_ref[rows, :].astype(MM_DTYPE), wple_ref[...])
        ya = _dot(ya_ref[rows, :], woa_ref[...])
        g0 = g_ref[rows, :D_MODEL].astype(jnp.float32)
        g1 = g_ref[rows, D_MODEL:].astype(jnp.float32)
        merged = g0 * yr_ref[rows, :].astype(jnp.float32) + g1 * ya
        x2 = x_ref[rows, :] + _dot(merged.astype(MM_DTYPE), wout_ref[...])
        staged.append((x2, pe))
    for rows, (x2, pe) in zip(halves, staged):
        n2 = (_rms_normalize(x2) * np_ref[...]).astype(MM_DTYPE)
        pg = _sigmoid(_dot(n2, wpg_ref[...]) + bpg_ref[...])
        o_ref[rows, :] = x2 + pg * pe


def _out_proj(x, p, ya, yr, gates, woa, wout, npl, wpg, bpg, wple):
    B, S, _ = x.shape
    tm = TM_OUT
    row = lambda w: pl.BlockSpec((None, tm, w), lambda b, i: (b, i, 0))
    return pl.pallas_call(
        _out_kernel,
        grid=(B, S // tm),
        in_specs=[
            row(D_MODEL), row(PLE_DIM), row(ATT_W), row(D_MODEL), row(2 * D_MODEL),
            _const_spec(woa.shape), _const_spec(wout.shape), _const_spec(npl.shape),
            _const_spec(wpg.shape), _const_spec(bpg.shape), _const_spec(wple.shape),
        ],
        out_specs=row(D_MODEL),
        out_shape=jax.ShapeDtypeStruct((B, S, D_MODEL), x.dtype),
        compiler_params=_params(2),
        name="out_proj",
    )(x, p, ya, yr, gates, woa, wout, npl, wpg, bpg, wple)


def _rope_tables(s):
    ang = (np.arange(s, dtype=np.float64)[:, None]
           * ROPE_THETA ** (-np.arange(0, HEAD_DIM, 2, dtype=np.float64) / HEAD_DIM)[None, :])
    cos, sin = np.cos(ang), np.sin(ang)
    cos2 = np.concatenate([cos, cos], axis=1).astype(np.float32)
    sin2 = np.concatenate([-sin, sin], axis=1).astype(np.float32)
    tabs = []
    for _, dil in ATTN_PATTERNS:
        if dil == 1:
            tabs.append((jnp.asarray(cos2), jnp.asarray(sin2)))
        else:
            perm = lambda t: jnp.asarray(
                np.ascontiguousarray(t.reshape(s // dil, dil, HEAD_DIM).transpose(1, 0, 2)))
            tabs.append((perm(cos2), perm(sin2)))
    return tabs


def kernel(x, p, norm_mix, w_in, b_in, conv_w, conv_b, w_rg_a, b_rg_a, w_rg_x, b_rg_x,
           lru_lambda, q_norm, k_norm, w_o_rnn, w_o_att, w_out, norm_ple, w_ple_gate,
           b_ple_gate, w_ple):
    depth = w_in.shape[0]
    s = x.shape[1]
    tabs = _rope_tables(s)
    f32 = jnp.float32
    for layer in range(depth):
        w = w_in[layer]
        b = b_in[layer].astype(f32)[None, :]
        nm = norm_mix[layer].astype(f32)[None, :]
        cols = lambda a, lo, hi: a[:, lo:hi]
        grp = lambda a, g: cols(a, OFF_QKV + g * GROUP_W, OFF_QKV + (g + 1) * GROUP_W)
        wb = lambda lo, hi: (cols(w, lo, hi).astype(MM_DTYPE), cols(b, lo, hi))
        gq = q_norm[layer].astype(f32) * (HEAD_DIM ** -0.5 * LOG2_E)
        gk = k_norm[layer].astype(f32)
        attn_w = (grp(w, 0).astype(MM_DTYPE), grp(b, 0),
                  *wb(OFF_Z_ATT, OFF_GATES), *wb(OFF_GATES, w.shape[1]),
                  grp(w, 1).astype(MM_DTYPE), grp(b, 1),
                  grp(w, 2).astype(MM_DTYPE), grp(b, 2), gq, gk)
        wg = jnp.concatenate([w_rg_a[layer], w_rg_x[layer]], axis=2).astype(MM_DTYPE)
        bg = jnp.stack([b_rg_a[layer], b_rg_x[layer]], axis=0).astype(f32)
        rnn_w = (*wb(0, OFF_Z_RNN), *wb(OFF_Z_RNN, OFF_QKV), conv_w[layer].astype(f32),
                 conv_b[layer].astype(f32)[None, :], wg, bg,
                 lru_lambda[layer].astype(f32)[None, :], w_o_rnn[layer].astype(MM_DTYPE))
        qkv0, qkv1, qkv2, sz, gates = _in_proj_attn(x, nm, attn_w, tabs)
        yr = _rnn_branch(x, nm, rnn_w)
        ya = _attention(qkv0, qkv1, qkv2, sz)
        x = _out_proj(
            x, p[layer], ya, yr, gates,
            w_o_att[layer].astype(MM_DTYPE), w_out[layer].astype(MM_DTYPE),
            norm_ple[layer].astype(f32)[None, :], w_ple_gate[layer].astype(MM_DTYPE),
            b_ple_gate[layer].astype(f32)[None, :], w_ple[layer].astype(MM_DTYPE))
    return x
```

```python
import functools
import math

import jax
import jax.numpy as jnp
import numpy as np
from jax import lax
from jax.experimental import pallas as pl
from jax.experimental.pallas import tpu as pltpu

D_MODEL = 1024
PLE_DIM = 256
D_RNN = 1280
RNN_BLOCKS = 10
RNN_BLOCK_W = D_RNN // RNN_BLOCKS
CONV_W = 4
LRU_C = 8.0
HEAD_DIM = 128
HEADS = 4
ATTN_PATTERNS = ((128, 1), (512, 4), (2048, 16))
N_GROUPS = len(ATTN_PATTERNS)
ATT_W = HEADS * HEAD_DIM
GROUP_W = 3 * ATT_W
ATT_BLK = 128
ROPE_THETA = 10000.0
EPS = 1e-6

OFF_Z_RNN = D_RNN
OFF_QKV = 2 * D_RNN
OFF_Z_ATT = OFF_QKV + N_GROUPS * GROUP_W
OFF_GATES = OFF_Z_ATT + ATT_W

LANES = 128
SUBLANES = 8
VMEM_LIMIT_BYTES = 56 * 1024 * 1024

TM_IN = 512
IN_SUB = 2
TT_RNN = 64
RNN_SUB_T = 16
TM_OUT = 1024
OUT_SUB = 4
COMBINE_ROWS = 256

MM_DTYPE = jnp.bfloat16
LOG2_E = math.log2(math.e)
LN_2 = math.log(2.0)
NEG_INF = float("-inf")


def _sigmoid(v):
    return 1.0 / (1.0 + jnp.exp2(v * (-LOG2_E)))


def _rms_normalize(v):
    var = jnp.mean(v * v, axis=-1, keepdims=True)
    return v * lax.rsqrt(var + EPS)


def _dot(a, b):
    return jnp.dot(a, b, preferred_element_type=jnp.float32)


def _const_spec(shape):
    zeros = (0,) * len(shape)
    return pl.BlockSpec(shape, lambda *_: zeros, pipeline_mode=pl.Buffered(1))


def _params(n_grid):
    return pltpu.CompilerParams(
        dimension_semantics=("arbitrary",) * n_grid,
        vmem_limit_bytes=VMEM_LIMIT_BYTES,
    )


def _qk_epilogue(acc, gq, gk, cos2, sin2):
    outs = []
    for part, gain in ((0, gq), (1, gk)):
        for h in range(HEADS):
            lo = part * ATT_W + h * HEAD_DIM
            t = _rms_normalize(acc[:, lo:lo + HEAD_DIM]) * gain
            outs.append(t * cos2 + pltpu.roll(t, HEAD_DIM // 2, 1) * sin2)
    outs.append(acc[:, 2 * ATT_W:])
    return jnp.concatenate(outs, axis=1)


def _attn_natural(hb, rows, wq0_ref, bq0_ref, wz_ref, bz_ref, wgt_ref, bgt_ref, gq_ref, gk_ref,
                  cos0_ref, sin0_ref, qkv0_ref, sz_ref, gates_ref):
    acc = _dot(hb, wq0_ref[...]) + bq0_ref[...]
    qkv0_ref[rows, :] = _qk_epilogue(acc, gq_ref[0:1], gk_ref[0:1], cos0_ref[rows, :],
                                     sin0_ref[rows, :]).astype(qkv0_ref.dtype)
    z = _dot(hb, wz_ref[...]) + bz_ref[...]
    sz_ref[rows, :] = (z * _sigmoid(z)).astype(sz_ref.dtype)
    g = _dot(hb, wgt_ref[...]) + bgt_ref[...]
    gates_ref[rows, :] = _sigmoid(g).astype(gates_ref.dtype)


def _attn_dilated(hn, sub, w1_ref, b1_ref, w2_ref, b2_ref, gq_ref, gk_ref, cos1_ref, sin1_ref,
                  cos2_ref, sin2_ref, qkv1_ref, qkv2_ref, hs_ref):
    tm = hn.shape[0]
    n_slabs = D_MODEL // LANES
    for j in range(n_slabs):
        hs_ref[sub, j] = hn[:, j * LANES:(j + 1) * LANES]
    for (dil, w_ref, b_ref, gi, cos_ref, sin_ref, out_ref) in (
            (ATTN_PATTERNS[1][1], w1_ref, b1_ref, 1, cos1_ref, sin1_ref, qkv1_ref),
            (ATTN_PATTERNS[2][1], w2_ref, b2_ref, 2, cos2_ref, sin2_ref, qkv2_ref)):
        per = tm // dil
        classes = []
        for c in range(dil):
            classes.append(jnp.concatenate(
                [hs_ref[sub, j, pl.ds(c, per, stride=dil), :] for j in range(n_slabs)],
                axis=1))
        hp = jnp.concatenate(classes, axis=0).astype(MM_DTYPE)
        acc = _dot(hp, w_ref[...]) + b_ref[...]
        part = pl.ds(sub * per, per)
        cos2 = cos_ref[:, part, :].reshape(tm, HEAD_DIM)
        sin2 = sin_ref[:, part, :].reshape(tm, HEAD_DIM)
        res = _qk_epilogue(acc, gq_ref[gi:gi + 1], gk_ref[gi:gi + 1], cos2, sin2)
        out_ref[:, part, :] = res.astype(out_ref.dtype).reshape(dil, per, GROUP_W)


def _in_proj_attn_kernel(x_ref, nm_ref,
                         wq0_ref, bq0_ref, wz_ref, bz_ref, wgt_ref, bgt_ref, w1_ref, b1_ref,
                         w2_ref, b2_ref, gq_ref, gk_ref,
                         cos0_ref, sin0_ref, cos1_ref, sin1_ref, cos2_ref, sin2_ref,
                         qkv0_ref, qkv1_ref, qkv2_ref, sz_ref, gates_ref, hs_ref):
    ts = x_ref.shape[0] // IN_SUB
    for sub in range(IN_SUB):
        rows = pl.ds(sub * ts, ts)
        hn = _rms_normalize(x_ref[rows, :]) * nm_ref[...]
        _attn_natural(hn.astype(MM_DTYPE), rows, wq0_ref, bq0_ref, wz_ref, bz_ref, wgt_ref,
                      bgt_ref, gq_ref, gk_ref, cos0_ref, sin0_ref, qkv0_ref, sz_ref, gates_ref)
        _attn_dilated(hn, sub, w1_ref, b1_ref, w2_ref, b2_ref, gq_ref, gk_ref, cos1_ref,
                      sin1_ref, cos2_ref, sin2_ref, qkv1_ref, qkv2_ref, hs_ref)


def _in_proj_attn(x, nm, attn_w, tabs):
    B, S, _ = x.shape
    tm = TM_IN
    d1, d2 = ATTN_PATTERNS[1][1], ATTN_PATTERNS[2][1]
    (cos0, sin0), (cos1, sin1), (cos2, sin2) = tabs
    row = lambda w: pl.BlockSpec((None, tm, w), lambda b, i: (b, i, 0))
    tab_specs = [
        pl.BlockSpec((tm, HEAD_DIM), lambda b, i: (i, 0)),
        pl.BlockSpec((tm, HEAD_DIM), lambda b, i: (i, 0)),
        pl.BlockSpec((d1, tm // d1, HEAD_DIM), lambda b, i: (0, i, 0)),
        pl.BlockSpec((d1, tm // d1, HEAD_DIM), lambda b, i: (0, i, 0)),
        pl.BlockSpec((d2, tm // d2, HEAD_DIM), lambda b, i: (0, i, 0)),
        pl.BlockSpec((d2, tm // d2, HEAD_DIM), lambda b, i: (0, i, 0)),
    ]
    in_specs = ([row(D_MODEL), _const_spec(nm.shape)]
                + [_const_spec(a.shape) for a in attn_w] + tab_specs)
    out_shape = (
        jax.ShapeDtypeStruct((B, S, GROUP_W), MM_DTYPE),
        jax.ShapeDtypeStruct((B, d1, S // d1, GROUP_W), MM_DTYPE),
        jax.ShapeDtypeStruct((B, d2, S // d2, GROUP_W), MM_DTYPE),
        jax.ShapeDtypeStruct((B, S, ATT_W), MM_DTYPE),
        jax.ShapeDtypeStruct((B, S, 2 * D_MODEL), MM_DTYPE),
    )
    out_specs = (
        row(GROUP_W),
        pl.BlockSpec((None, d1, tm // d1, GROUP_W), lambda b, i: (b, 0, i, 0)),
        pl.BlockSpec((None, d2, tm // d2, GROUP_W), lambda b, i: (b, 0, i, 0)),
        row(ATT_W),
        row(2 * D_MODEL),
    )
    return pl.pallas_call(
        _in_proj_attn_kernel,
        grid=(B, S // tm),
        in_specs=in_specs,
        out_specs=out_specs,
        out_shape=out_shape,
        scratch_shapes=[pltpu.VMEM((IN_SUB, D_MODEL // LANES, tm // IN_SUB, LANES),
                                   jnp.float32)],
        compiler_params=_params(2),
        name="in_proj_attn",
    )(x, nm, *attn_w, cos0, sin0, cos1, sin1, cos2, sin2)


def _rnn_kernel(x_ref, nm_ref, perm_ref, wx_ref, bx_ref, wz_ref, bz_ref, cw_ref, cb_ref, wg_ref,
                bg_ref, lam_ref, wo_ref, yr_ref, xs_ref, h_ref):
    nb, tt, _ = x_ref.shape
    ts = RNN_SUB_T
    n_sub = tt // ts
    rows = nb * ts
    hist = (CONV_W - 1) * nb
    step = pl.program_id(0)

    @pl.when(step == 0)
    def _():
        xs_ref[0:hist, :] = jnp.zeros((hist, D_RNN), jnp.float32)
        h_ref[...] = jnp.zeros_like(h_ref)

    neg_lam = -lam_ref[...]
    softplus = jnp.maximum(neg_lam, 0.0) + jnp.log(1.0 + jnp.exp(-jnp.abs(neg_lam)))
    log2_a_scale = (-LRU_C * LOG2_E) * softplus

    def front(s):
        hn = jnp.concatenate(
            [_rms_normalize(x_ref[b, s * ts:(s + 1) * ts, :]) for b in range(nb)],
            axis=0) * nm_ref[...]
        hp = _dot(perm_ref[0], hn.astype(MM_DTYPE)).astype(MM_DTYPE)
        xr = _dot(hp, wx_ref[...]) + bx_ref[...]
        z = _dot(hp, wz_ref[...]) + bz_ref[...]
        xs_ref[pl.ds(hist + s * rows, rows), :] = xr
        xc = cb_ref[...]
        for k in range(CONV_W):
            xc = xc + cw_ref[k:k + 1, :] * xs_ref[pl.ds(s * rows + k * nb, rows), :]
        xcb = xc.astype(MM_DTYPE)
        pre = [_dot(xcb[:, n * RNN_BLOCK_W:(n + 1) * RNN_BLOCK_W], wg_ref[n])
               for n in range(RNN_BLOCKS)]
        return xc, pre, z

    def back(s, staged, h):
        xc, pre, z = staged
        r = _sigmoid(jnp.concatenate([t[:, :RNN_BLOCK_W] for t in pre], axis=1)
                     + bg_ref[0:1, :])
        gi = _sigmoid(jnp.concatenate([t[:, RNN_BLOCK_W:] for t in pre], axis=1)
                      + bg_ref[1:2, :])
        a = jnp.exp2(log2_a_scale * r)
        y1 = 1.0 - a * a
        mult = jnp.where(y1 > 0.0, y1 * lax.rsqrt(y1), 0.0)
        gx = gi * xc
        hs = []
        for t in range(ts):
            sl = slice(t * nb, (t + 1) * nb)
            m_t = mult[sl]
            if s == 0 and t == 0:
                m_t = jnp.where(step == 0, 1.0, m_t)
            h = a[sl] * h + m_t * gx[sl]
            hs.append(h)
        y = (jnp.concatenate(hs, axis=0) * (z * _sigmoid(z))).astype(MM_DTYPE)
        yr = _dot(y, wo_ref[...])
        yn = _dot(perm_ref[1], yr.astype(yr_ref.dtype)).astype(yr_ref.dtype)
        for b in range(nb):
            yr_ref[b, s * ts:(s + 1) * ts, :] = yn[b * ts:(b + 1) * ts]
        return h

    h = h_ref[...]
    staged = {0: front(0)}
    for s in range(n_sub):
        if s + 1 < n_sub:
            staged[s + 1] = front(s + 1)
        h = back(s, staged.pop(s), h)
    h_ref[...] = h
    xs_ref[0:hist, :] = xs_ref[pl.ds(n_sub * rows, hist), :]


def _time_major_perms(nb, ts):
    r = np.arange(nb * ts)
    p = np.zeros((nb * ts, nb * ts), np.float32)
    p[r, (r % nb) * ts + r // nb] = 1.0
    return jnp.asarray(np.stack([p, p.T]), MM_DTYPE)


def _rnn_branch(x, nm, rnn_w):
    B, S, _ = x.shape
    tt = TT_RNN
    perms = _time_major_perms(B, RNN_SUB_T)
    return pl.pallas_call(
        _rnn_kernel,
        grid=(S // tt,),
        in_specs=([pl.BlockSpec((B, tt, D_MODEL), lambda i: (0, i, 0)), _const_spec(nm.shape),
                   _const_spec(perms.shape)]
                  + [_const_spec(a.shape) for a in rnn_w]),
        out_specs=pl.BlockSpec((B, tt, D_MODEL), lambda i: (0, i, 0)),
        out_shape=jax.ShapeDtypeStruct((B, S, D_MODEL), MM_DTYPE),
        scratch_shapes=[
            pltpu.VMEM(((CONV_W - 1) * B + B * tt, D_RNN), jnp.float32),
            pltpu.VMEM((B, D_RNN), jnp.float32),
        ],
        compiler_params=_params(1),
        name="rnn_branch",
    )(x, nm, perms, *rnn_w)


def _attn_kernel(q0, k0, v0, q1, k1, v1, q2, k2, v2, sz_ref, y_ref, o_nat, l_nat):
    S = sz_ref.shape[0]
    n_blk = S // ATT_BLK
    blk3 = (n_blk, ATT_BLK, HEAD_DIM)
    qi = lax.broadcasted_iota(jnp.int32, (1, ATT_BLK, ATT_BLK), 1)
    kj = lax.broadcasted_iota(jnp.int32, (1, ATT_BLK, ATT_BLK), 2)
    cur_ok = kj <= qi
    blk_id = lax.broadcasted_iota(jnp.int32, (n_blk, 1, 1), 0)
    qk_dims = (((2,), (2,)), ((0,), (0,)))
    pv_dims = (((2,), (1,)), ((0,), (0,)))

    for g, (q_ref, k_ref, v_ref) in enumerate(((q0, k0, v0), (q1, k1, v1), (q2, k2, v2))):
        dil = ATTN_PATTERNS[g][1]
        blocks_per_class = n_blk // dil
        q = q_ref[...].reshape(blk3)
        k = k_ref[...].reshape(blk3)
        v = v_ref[...].reshape(blk3)
        v1 = jnp.concatenate([v, jnp.ones_like(v)], axis=2)
        if blocks_per_class > 1:
            shift = lambda t: jnp.concatenate([t[:1], t[:-1]], axis=0)
            kk = jnp.concatenate([shift(k), k], axis=1)
            vv = jnp.concatenate([shift(v1), v1], axis=1)
            s = lax.dot_general(q, kk, qk_dims, preferred_element_type=jnp.float32)
            first = (blk_id & (blocks_per_class - 1)) == 0
            prev_ok = kj >= qi + jnp.where(first, ATT_BLK, 0)
            s_prev = jnp.where(prev_ok, s[:, :, :ATT_BLK], NEG_INF)
            s_cur = jnp.where(cur_ok, s[:, :, ATT_BLK:], NEG_INF)
            m = jnp.max(jnp.maximum(s_prev, s_cur), axis=-1, keepdims=True)
            e = jnp.concatenate([jnp.exp2(s_prev - m), jnp.exp2(s_cur - m)], axis=2)
        else:
            vv = v1
            s = lax.dot_general(q, k, qk_dims, preferred_element_type=jnp.float32)
            s = jnp.where(cur_ok, s, NEG_INF)
            m = jnp.max(s, axis=-1, keepdims=True)
            e = jnp.exp2(s - m)
        od = lax.dot_general(e.astype(MM_DTYPE), vv, pv_dims,
                             preferred_element_type=jnp.float32)
        den = od[:, :, HEAD_DIM:]
        o = od[:, :, :HEAD_DIM] * (1.0 / den)
        lse = m * LN_2 + jnp.log(den)
        if dil == 1:
            o_nat[g] = o.reshape(S, HEAD_DIM)
            l_nat[g] = lse.reshape(S, HEAD_DIM)
        else:
            for nb in range(n_blk):
                c, m0 = divmod(nb, blocks_per_class)
                idx = pl.ds(m0 * ATT_BLK * dil + c, ATT_BLK, stride=dil)
                o_nat[g, idx, :] = o[nb]
                l_nat[g, idx, :] = lse[nb]

    def merge(i, carry):
        rows = pl.ds(pl.multiple_of(i * COMBINE_ROWS, COMBINE_ROWS), COMBINE_ROWS)
        ls = [l_nat[g, rows, :] for g in range(N_GROUPS)]
        m = functools.reduce(jnp.maximum, ls)
        ws = [jnp.exp(l - m) for l in ls]
        den = functools.reduce(lambda p, q: p + q, ws)
        att = functools.reduce(
            lambda p, q: p + q, [w * o_nat[g, rows, :] for g, w in enumerate(ws)])
        att = att * (1.0 / den)
        y_ref[rows, :] = (att * sz_ref[rows, :].astype(jnp.float32)).astype(y_ref.dtype)
        return carry
    lax.fori_loop(0, S // COMBINE_ROWS, merge, 0)


def _attention(qkv0, qkv1, qkv2, sz):
    B, S, _ = qkv0.shape
    qkv1 = qkv1.reshape(B, S, GROUP_W)
    qkv2 = qkv2.reshape(B, S, GROUP_W)
    in_specs = []
    for _ in range(N_GROUPS):
        for part in range(3):
            in_specs.append(pl.BlockSpec((None, S, HEAD_DIM),
                                         lambda b, h, part=part: (b, 0, part * HEADS + h)))
    in_specs.append(pl.BlockSpec((None, S, HEAD_DIM), lambda b, h: (b, 0, h)))
    return pl.pallas_call(
        _attn_kernel,
        grid=(B, HEADS),
        in_specs=in_specs,
        out_specs=pl.BlockSpec((None, S, HEAD_DIM), lambda b, h: (b, 0, h)),
        out_shape=jax.ShapeDtypeStruct((B, S, ATT_W), MM_DTYPE),
        scratch_shapes=[
            pltpu.VMEM((N_GROUPS, S, HEAD_DIM), jnp.float32),
            pltpu.VMEM((N_GROUPS, S, HEAD_DIM), jnp.float32),
        ],
        compiler_params=_params(2),
        name="attention",
    )(qkv0, qkv0, qkv0, qkv1, qkv1, qkv1, qkv2, qkv2, qkv2, sz)


def _out_kernel(x_ref, p_ref, ya_ref, yr_ref, g_ref, woa_ref, wout_ref, np_ref, wpg_ref,
                bpg_ref, wple_ref, o_ref):
    tm = x_ref.shape[0]
    halves = [pl.ds(i * (tm // OUT_SUB), tm // OUT_SUB) for i in range(OUT_SUB)]
    staged = []
    for rows in halves:
        pe = _dot(p_ref[rows, :].astype(MM_DTYPE), wple_ref[...])
        ya = _dot(ya_ref[rows, :], woa_ref[...])
        g0 = g_ref[rows, :D_MODEL].astype(jnp.float32)
        g1 = g_ref[rows, D_MODEL:].astype(jnp.float32)
        merged = g0 * yr_ref[rows, :].astype(jnp.float32) + g1 * ya
        x2 = x_ref[rows, :] + _dot(merged.astype(MM_DTYPE), wout_ref[...])
        staged.append((x2, pe))
    for rows, (x2, pe) in zip(halves, staged):
        n2 = (_rms_normalize(x2) * np_ref[...]).astype(MM_DTYPE)
        pg = _sigmoid(_dot(n2, wpg_ref[...]) + bpg_ref[...])
        o_ref[rows, :] = x2 + pg * pe


def _out_proj(x, p, ya, yr, gates, woa, wout, npl, wpg, bpg, wple):
    B, S, _ = x.shape
    tm = TM_OUT
    row = lambda w: pl.BlockSpec((None, tm, w), lambda b, i: (b, i, 0))
    return pl.pallas_call(
        _out_kernel,
        grid=(B, S // tm),
        in_specs=[
            row(D_MODEL), row(PLE_DIM), row(ATT_W), row(D_MODEL), row(2 * D_MODEL),
            _const_spec(woa.shape), _const_spec(wout.shape), _const_spec(npl.shape),
            _const_spec(wpg.shape), _const_spec(bpg.shape), _const_spec(wple.shape),
        ],
        out_specs=row(D_MODEL),
        out_shape=jax.ShapeDtypeStruct((B, S, D_MODEL), x.dtype),
        compiler_params=_params(2),
        name="out_proj",
    )(x, p, ya, yr, gates, woa, wout, npl, wpg, bpg, wple)


def _rope_tables(s):
    ang = (np.arange(s, dtype=np.float64)[:, None]
           * ROPE_THETA ** (-np.arange(0, HEAD_DIM, 2, dtype=np.float64) / HEAD_DIM)[None, :])
    cos, sin = np.cos(ang), np.sin(ang)
    cos2 = np.concatenate([cos, cos], axis=1).astype(np.float32)
    sin2 = np.concatenate([-sin, sin], axis=1).astype(np.float32)
    tabs = []
    for _, dil in ATTN_PATTERNS:
        if dil == 1:
            tabs.append((jnp.asarray(cos2), jnp.asarray(sin2)))
        else:
            perm = lambda t: jnp.asarray(
                np.ascontiguousarray(t.reshape(s // dil, dil, HEAD_DIM).transpose(1, 0, 2)))
            tabs.append((perm(cos2), perm(sin2)))
    return tabs


def kernel(x, p, norm_mix, w_in, b_in, conv_w, conv_b, w_rg_a, b_rg_a, w_rg_x, b_rg_x,
           lru_lambda, q_norm, k_norm, w_o_rnn, w_o_att, w_out, norm_ple, w_ple_gate,
           b_ple_gate, w_ple):
    depth = w_in.shape[0]
    s = x.shape[1]
    tabs = _rope_tables(s)
    f32 = jnp.float32
    for layer in range(depth):
        w = w_in[layer]
        b = b_in[layer].astype(f32)[None, :]
        nm = norm_mix[layer].astype(f32)[None, :]
        cols = lambda a, lo, hi: a[:, lo:hi]
        grp = lambda a, g: cols(a, OFF_QKV + g * GROUP_W, OFF_QKV + (g + 1) * GROUP_W)
        wb = lambda lo, hi: (cols(w, lo, hi).astype(MM_DTYPE), cols(b, lo, hi))
        gq = q_norm[layer].astype(f32) * (HEAD_DIM ** -0.5 * LOG2_E)
        gk = k_norm[layer].astype(f32)
        attn_w = (grp(w, 0).astype(MM_DTYPE), grp(b, 0),
                  *wb(OFF_Z_ATT, OFF_GATES), *wb(OFF_GATES, w.shape[1]),
                  grp(w, 1).astype(MM_DTYPE), grp(b, 1),
                  grp(w, 2).astype(MM_DTYPE), grp(b, 2), gq, gk)
        wg = jnp.concatenate([w_rg_a[layer], w_rg_x[layer]], axis=2).astype(MM_DTYPE)
        bg = jnp.stack([b_rg_a[layer], b_rg_x[layer]], axis=0).astype(f32)
        rnn_w = (*wb(0, OFF_Z_RNN), *wb(OFF_Z_RNN, OFF_QKV), conv_w[layer].astype(f32),
                 conv_b[layer].astype(f32)[None, :], wg, bg,
                 lru_lambda[layer].astype(f32)[None, :], w_o_rnn[layer].astype(MM_DTYPE))
        qkv0, qkv1, qkv2, sz, gates = _in_proj_attn(x, nm, attn_w, tabs)
        yr = _rnn_branch(x, nm, rnn_w)
        ya = _attention(qkv0, qkv1, qkv2, sz)
        x = _out_proj(
            x, p[layer], ya, yr, gates,
            w_o_att[layer].astype(MM_DTYPE), w_out[layer].astype(MM_DTYPE),
            norm_ple[layer].astype(f32)[None, :], w_ple_gate[layer].astype(MM_DTYPE),
            b_ple_gate[layer].astype(f32)[None, :], w_ple[layer].astype(MM_DTYPE))
    return x
```

```python
import functools
import math

import jax
import jax.numpy as jnp
import numpy as np
from jax import lax
from jax.experimental import pallas as pl
from jax.experimental.pallas import tpu as pltpu

D_MODEL = 1024
PLE_DIM = 256
D_RNN = 1280
RNN_BLOCKS = 10
RNN_BLOCK_W = D_RNN // RNN_BLOCKS
CONV_W = 4
LRU_C = 8.0
HEAD_DIM = 128
HEADS = 4
ATTN_PATTERNS = ((128, 1), (512, 4), (2048, 16))
N_GROUPS = len(ATTN_PATTERNS)
ATT_W = HEADS * HEAD_DIM
GROUP_W = 3 * ATT_W
ATT_BLK = 128
ROPE_THETA = 10000.0
EPS = 1e-6

OFF_Z_RNN = D_RNN
OFF_QKV = 2 * D_RNN
OFF_Z_ATT = OFF_QKV + N_GROUPS * GROUP_W
OFF_GATES = OFF_Z_ATT + ATT_W

LANES = 128
SUBLANES = 8
VMEM_LIMIT_BYTES = 56 * 1024 * 1024

TM_IN = 512
IN_SUB = 2
TT_RNN = 64
RNN_SUB_T = 16
TM_OUT = 1024
OUT_SUB = 4
COMBINE_ROWS = 256

MM_DTYPE = jnp.bfloat16
LOG2_E = math.log2(math.e)
NEG_INF = float("-inf")


def _sigmoid(v):
    return 1.0 / (1.0 + jnp.exp2(v * (-LOG2_E)))


def _rms_normalize(v):
    var = jnp.mean(v * v, axis=-1, keepdims=True)
    return v * lax.rsqrt(var + EPS)


def _dot(a, b):
    return jnp.dot(a, b, preferred_element_type=jnp.float32)


def _const_spec(shape):
    zeros = (0,) * len(shape)
    return pl.BlockSpec(shape, lambda *_: zeros, pipeline_mode=pl.Buffered(1))


def _params(n_grid):
    return pltpu.CompilerParams(
        dimension_semantics=("arbitrary",) * n_grid,
        vmem_limit_bytes=VMEM_LIMIT_BYTES,
    )


def _qk_epilogue(acc, gq, gk, cos2, sin2):
    outs = []
    for part, gain in ((0, gq), (1, gk)):
        for h in range(HEADS):
            lo = part * ATT_W + h * HEAD_DIM
            t = _rms_normalize(acc[:, lo:lo + HEAD_DIM]) * gain
            outs.append(t * cos2 + pltpu.roll(t, HEAD_DIM // 2, 1) * sin2)
    outs.append(acc[:, 2 * ATT_W:])
    return jnp.concatenate(outs, axis=1)


def _attn_natural(hb, rows, wq0_ref, bq0_ref, wz_ref, bz_ref, wgt_ref, bgt_ref, gq_ref, gk_ref,
                  cos0_ref, sin0_ref, qkv0_ref, sz_ref, gates_ref):
    acc = _dot(hb, wq0_ref[...]) + bq0_ref[...]
    qkv0_ref[rows, :] = _qk_epilogue(acc, gq_ref[0:1], gk_ref[0:1], cos0_ref[rows, :],
                                     sin0_ref[rows, :]).astype(qkv0_ref.dtype)
    z = _dot(hb, wz_ref[...]) + bz_ref[...]
    sz_ref[rows, :] = (z * _sigmoid(z)).astype(sz_ref.dtype)
    g = _dot(hb, wgt_ref[...]) + bgt_ref[...]
    gates_ref[rows, :] = _sigmoid(g).astype(gates_ref.dtype)


def _attn_dilated(hn, sub, w1_ref, b1_ref, w2_ref, b2_ref, gq_ref, gk_ref, cos1_ref, sin1_ref,
                  cos2_ref, sin2_ref, qkv1_ref, qkv2_ref, hs_ref):
    tm = hn.shape[0]
    n_slabs = D_MODEL // LANES
    for j in range(n_slabs):
        hs_ref[sub, j] = hn[:, j * LANES:(j + 1) * LANES]
    for (dil, w_ref, b_ref, gi, cos_ref, sin_ref, out_ref) in (
            (ATTN_PATTERNS[1][1], w1_ref, b1_ref, 1, cos1_ref, sin1_ref, qkv1_ref),
            (ATTN_PATTERNS[2][1], w2_ref, b2_ref, 2, cos2_ref, sin2_ref, qkv2_ref)):
        per = tm // dil
        classes = []
        for c in range(dil):
            classes.append(jnp.concatenate(
                [hs_ref[sub, j, pl.ds(c, per, stride=dil), :] for j in range(n_slabs)],
                axis=1))
        hp = jnp.concatenate(classes, axis=0).astype(MM_DTYPE)
        acc = _dot(hp, w_ref[...]) + b_ref[...]
        part = pl.ds(sub * per, per)
        cos2 = cos_ref[:, part, :].reshape(tm, HEAD_DIM)
        sin2 = sin_ref[:, part, :].reshape(tm, HEAD_DIM)
        res = _qk_epilogue(acc, gq_ref[gi:gi + 1], gk_ref[gi:gi + 1], cos2, sin2)
        out_ref[:, part, :] = res.astype(out_ref.dtype).reshape(dil, per, GROUP_W)


def _in_proj_attn_kernel(x_ref,
                         wq0_ref, bq0_ref, wz_ref, bz_ref, wgt_ref, bgt_ref, w1_ref, b1_ref,
                         w2_ref, b2_ref, gq_ref, gk_ref,
                         cos0_ref, sin0_ref, cos1_ref, sin1_ref, cos2_ref, sin2_ref,
                         qkv0_ref, qkv1_ref, qkv2_ref, sz_ref, gates_ref, hs_ref):
    ts = x_ref.shape[0] // IN_SUB
    for sub in range(IN_SUB):
        rows = pl.ds(sub * ts, ts)
        hn = _rms_normalize(x_ref[rows, :])
        _attn_natural(hn.astype(MM_DTYPE), rows, wq0_ref, bq0_ref, wz_ref, bz_ref, wgt_ref,
                      bgt_ref, gq_ref, gk_ref, cos0_ref, sin0_ref, qkv0_ref, sz_ref, gates_ref)
        _attn_dilated(hn, sub, w1_ref, b1_ref, w2_ref, b2_ref, gq_ref, gk_ref, cos1_ref,
                      sin1_ref, cos2_ref, sin2_ref, qkv1_ref, qkv2_ref, hs_ref)


def _in_proj_attn(x, attn_w, tabs):
    B, S, _ = x.shape
    tm = TM_IN
    d1, d2 = ATTN_PATTERNS[1][1], ATTN_PATTERNS[2][1]
    (cos0, sin0), (cos1, sin1), (cos2, sin2) = tabs
    row = lambda w: pl.BlockSpec((None, tm, w), lambda b, i: (b, i, 0))
    tab_specs = [
        pl.BlockSpec((tm, HEAD_DIM), lambda b, i: (i, 0)),
        pl.BlockSpec((tm, HEAD_DIM), lambda b, i: (i, 0)),
        pl.BlockSpec((d1, tm // d1, HEAD_DIM), lambda b, i: (0, i, 0)),
        pl.BlockSpec((d1, tm // d1, HEAD_DIM), lambda b, i: (0, i, 0)),
        pl.BlockSpec((d2, tm // d2, HEAD_DIM), lambda b, i: (0, i, 0)),
        pl.BlockSpec((d2, tm // d2, HEAD_DIM), lambda b, i: (0, i, 0)),
    ]
    in_specs = [row(D_MODEL)] + [_const_spec(a.shape) for a in attn_w] + tab_specs
    out_shape = (
        jax.ShapeDtypeStruct((B, S, GROUP_W), MM_DTYPE),
        jax.ShapeDtypeStruct((B, d1, S // d1, GROUP_W), MM_DTYPE),
        jax.ShapeDtypeStruct((B, d2, S // d2, GROUP_W), MM_DTYPE),
        jax.ShapeDtypeStruct((B, S, ATT_W), MM_DTYPE),
        jax.ShapeDtypeStruct((B, S, 2 * D_MODEL), MM_DTYPE),
    )
    out_specs = (
        row(GROUP_W),
        pl.BlockSpec((None, d1, tm // d1, GROUP_W), lambda b, i: (b, 0, i, 0)),
        pl.BlockSpec((None, d2, tm // d2, GROUP_W), lambda b, i: (b, 0, i, 0)),
        row(ATT_W),
        row(2 * D_MODEL),
    )
    return pl.pallas_call(
        _in_proj_attn_kernel,
        grid=(B, S // tm),
        in_specs=in_specs,
        out_specs=out_specs,
        out_shape=out_shape,
        scratch_shapes=[pltpu.VMEM((IN_SUB, D_MODEL // LANES, tm // IN_SUB, LANES),
                                   jnp.float32)],
        compiler_params=_params(2),
        name="in_proj_attn",
    )(x, *attn_w, cos0, sin0, cos1, sin1, cos2, sin2)


def _rnn_kernel(x_ref, perm_ref, wx_ref, bx_ref, wz_ref, bz_ref, cw_ref, cb_ref, wg_ref, bg_ref,
                lam_ref, wo_ref, yr_ref, xs_ref, h_ref):
    nb, tt, _ = x_ref.shape
    ts = RNN_SUB_T
    n_sub = tt // ts
    rows = nb * ts
    hist = (CONV_W - 1) * nb
    step = pl.program_id(0)

    @pl.when(step == 0)
    def _():
        xs_ref[0:hist, :] = jnp.zeros((hist, D_RNN), jnp.float32)
        h_ref[...] = jnp.zeros_like(h_ref)

    neg_lam = -lam_ref[...]
    softplus = jnp.maximum(neg_lam, 0.0) + jnp.log(1.0 + jnp.exp(-jnp.abs(neg_lam)))
    log2_a_scale = (-LRU_C * LOG2_E) * softplus

    def front(s):
        hn = jnp.concatenate(
            [_rms_normalize(x_ref[b, s * ts:(s + 1) * ts, :]) for b in range(nb)], axis=0)
        hp = _dot(perm_ref[0], hn.astype(MM_DTYPE)).astype(MM_DTYPE)
        xr = _dot(hp, wx_ref[...]) + bx_ref[...]
        z = _dot(hp, wz_ref[...]) + bz_ref[...]
        xs_ref[pl.ds(hist + s * rows, rows), :] = xr
        xc = cb_ref[...]
        for k in range(CONV_W):
            xc = xc + cw_ref[k:k + 1, :] * xs_ref[pl.ds(s * rows + k * nb, rows), :]
        xcb = xc.astype(MM_DTYPE)
        pre = [_dot(xcb[:, n * RNN_BLOCK_W:(n + 1) * RNN_BLOCK_W], wg_ref[n])
               for n in range(RNN_BLOCKS)]
        return xc, pre, z

    def back(s, staged, h):
        xc, pre, z = staged
        r = _sigmoid(jnp.concatenate([t[:, :RNN_BLOCK_W] for t in pre], axis=1)
                     + bg_ref[0:1, :])
        gi = _sigmoid(jnp.concatenate([t[:, RNN_BLOCK_W:] for t in pre], axis=1)
                      + bg_ref[1:2, :])
        a = jnp.exp2(log2_a_scale * r)
        y1 = 1.0 - a * a
        mult = jnp.where(y1 > 0.0, y1 * lax.rsqrt(y1), 0.0)
        gx = gi * xc
        hs = []
        for t in range(ts):
            sl = slice(t * nb, (t + 1) * nb)
            m_t = mult[sl]
            if s == 0 and t == 0:
                m_t = jnp.where(step == 0, 1.0, m_t)
            h = a[sl] * h + m_t * gx[sl]
            hs.append(h)
        y = (jnp.concatenate(hs, axis=0) * (z * _sigmoid(z))).astype(MM_DTYPE)
        yr = _dot(y, wo_ref[...])
        yn = _dot(perm_ref[1], yr.astype(yr_ref.dtype)).astype(yr_ref.dtype)
        for b in range(nb):
            yr_ref[b, s * ts:(s + 1) * ts, :] = yn[b * ts:(b + 1) * ts]
        return h

    h = h_ref[...]
    staged = {0: front(0)}
    for s in range(n_sub):
        if s + 1 < n_sub:
            staged[s + 1] = front(s + 1)
        h = back(s, staged.pop(s), h)
    h_ref[...] = h
    xs_ref[0:hist, :] = xs_ref[pl.ds(n_sub * rows, hist), :]


def _time_major_perms(nb, ts):
    r = np.arange(nb * ts)
    p = np.zeros((nb * ts, nb * ts), np.float32)
    p[r, (r % nb) * ts + r // nb] = 1.0
    return jnp.asarray(np.stack([p, p.T]), MM_DTYPE)


def _rnn_branch(x, rnn_w):
    B, S, _ = x.shape
    tt = TT_RNN
    perms = _time_major_perms(B, RNN_SUB_T)
    return pl.pallas_call(
        _rnn_kernel,
        grid=(S // tt,),
        in_specs=([pl.BlockSpec((B, tt, D_MODEL), lambda i: (0, i, 0)),
                   _const_spec(perms.shape)]
                  + [_const_spec(a.shape) for a in rnn_w]),
        out_specs=pl.BlockSpec((B, tt, D_MODEL), lambda i: (0, i, 0)),
        out_shape=jax.ShapeDtypeStruct((B, S, D_MODEL), MM_DTYPE),
        scratch_shapes=[
            pltpu.VMEM(((CONV_W - 1) * B + B * tt, D_RNN), jnp.float32),
            pltpu.VMEM((B, D_RNN), jnp.float32),
        ],
        compiler_params=_params(1),
        name="rnn_branch",
    )(x, perms, *rnn_w)


def _attn_kernel(q0, k0, v0, q1, k1, v1, q2, k2, v2, sz_ref, y_ref, o_nat, l_nat):
    S = sz_ref.shape[0]
    n_blk = S // ATT_BLK
    blk3 = (n_blk, ATT_BLK, HEAD_DIM)
    qi = lax.broadcasted_iota(jnp.int32, (1, ATT_BLK, ATT_BLK), 1)
    kj = lax.broadcasted_iota(jnp.int32, (1, ATT_BLK, ATT_BLK), 2)
    cur_ok = kj <= qi
    blk_id = lax.broadcasted_iota(jnp.int32, (n_blk, 1, 1), 0)
    qk_dims = (((2,), (2,)), ((0,), (0,)))
    pv_dims = (((2,), (1,)), ((0,), (0,)))

    for g, (q_ref, k_ref, v_ref) in enumerate(((q0, k0, v0), (q1, k1, v1), (q2, k2, v2))):
        dil = ATTN_PATTERNS[g][1]
        blocks_per_class = n_blk // dil
        q = q_ref[...].reshape(blk3)
        k = k_ref[...].reshape(blk3)
        v = v_ref[...].reshape(blk3)
        v1 = jnp.concatenate([v, jnp.ones_like(v)], axis=2)
        if blocks_per_class > 1:
            shift = lambda t: jnp.concatenate([t[:1], t[:-1]], axis=0)
            kk = jnp.concatenate([shift(k), k], axis=1)
            vv = jnp.concatenate([shift(v1), v1], axis=1)
            s = lax.dot_general(q, kk, qk_dims, preferred_element_type=jnp.float32)
            first = (blk_id & (blocks_per_class - 1)) == 0
            prev_ok = kj >= qi + jnp.where(first, ATT_BLK, 0)
            s_prev = jnp.where(prev_ok, s[:, :, :ATT_BLK], NEG_INF)
            s_cur = jnp.where(cur_ok, s[:, :, ATT_BLK:], NEG_INF)
            m = jnp.max(jnp.maximum(s_prev, s_cur), axis=-1, keepdims=True)
            e = jnp.concatenate([jnp.exp2(s_prev - m), jnp.exp2(s_cur - m)], axis=2)
        else:
            vv = v1
            s = lax.dot_general(q, k, qk_dims, preferred_element_type=jnp.float32)
            s = jnp.where(cur_ok, s, NEG_INF)
            m = jnp.max(s, axis=-1, keepdims=True)
            e = jnp.exp2(s - m)
        od = lax.dot_general(e.astype(MM_DTYPE), vv, pv_dims,
                             preferred_element_type=jnp.float32)
        den = od[:, :, HEAD_DIM:]
        o = od[:, :, :HEAD_DIM] * (1.0 / den)
        lse = m + jnp.log2(den)
        if dil == 1:
            o_nat[g] = o.reshape(S, HEAD_DIM)
            l_nat[g] = lse.reshape(S, HEAD_DIM)
        else:
            for nb in range(n_blk):
                c, m0 = divmod(nb, blocks_per_class)
                idx = pl.ds(m0 * ATT_BLK * dil + c, ATT_BLK, stride=dil)
                o_nat[g, idx, :] = o[nb]
                l_nat[g, idx, :] = lse[nb]

    def merge(i, carry):
        rows = pl.ds(pl.multiple_of(i * COMBINE_ROWS, COMBINE_ROWS), COMBINE_ROWS)
        ls = [l_nat[g, rows, :] for g in range(N_GROUPS)]
        m = functools.reduce(jnp.maximum, ls)
        ws = [jnp.exp2(l - m) for l in ls]
        den = functools.reduce(lambda p, q: p + q, ws)
        att = functools.reduce(
            lambda p, q: p + q, [w * o_nat[g, rows, :] for g, w in enumerate(ws)])
        att = att * (1.0 / den)
        y_ref[rows, :] = (att * sz_ref[rows, :].astype(jnp.float32)).astype(y_ref.dtype)
        return carry
    lax.fori_loop(0, S // COMBINE_ROWS, merge, 0)


def _attention(qkv0, qkv1, qkv2, sz):
    B, S, _ = qkv0.shape
    qkv1 = qkv1.reshape(B, S, GROUP_W)
    qkv2 = qkv2.reshape(B, S, GROUP_W)
    in_specs = []
    for _ in range(N_GROUPS):
        for part in range(3):
            in_specs.append(pl.BlockSpec((None, S, HEAD_DIM),
                                         lambda b, h, part=part: (b, 0, part * HEADS + h)))
    in_specs.append(pl.BlockSpec((None, S, HEAD_DIM), lambda b, h: (b, 0, h)))
    return pl.pallas_call(
        _attn_kernel,
        grid=(B, HEADS),
        in_specs=in_specs,
        out_specs=pl.BlockSpec((None, S, HEAD_DIM), lambda b, h: (b, 0, h)),
        out_shape=jax.ShapeDtypeStruct((B, S, ATT_W), MM_DTYPE),
        scratch_shapes=[
            pltpu.VMEM((N_GROUPS, S, HEAD_DIM), jnp.float32),
            pltpu.VMEM((N_GROUPS, S, HEAD_DIM), jnp.float32),
        ],
        compiler_params=_params(2),
        name="attention",
    )(qkv0, qkv0, qkv0, qkv1, qkv1, qkv1, qkv2, qkv2, qkv2, sz)


def _out_kernel(x_ref, p_ref, ya_ref, yr_ref, g_ref, woa_ref, wout_ref, wpg_ref, bpg_ref,
                wple_ref, o_ref):
    tm = x_ref.shape[0]
    halves = [pl.ds(i * (tm // OUT_SUB), tm // OUT_SUB) for i in range(OUT_SUB)]
    staged = []
    for rows in halves:
        pe = _dot(p_ref[rows, :].astype(MM_DTYPE), wple_ref[...])
        ya = _dot(ya_ref[rows, :], woa_ref[...])
        g0 = g_ref[rows, :D_MODEL].astype(jnp.float32)
        g1 = g_ref[rows, D_MODEL:].astype(jnp.float32)
        merged = g0 * yr_ref[rows, :].astype(jnp.float32) + g1 * ya
        x2 = x_ref[rows, :] + _dot(merged.astype(MM_DTYPE), wout_ref[...])
        staged.append((x2, pe))
    for rows, (x2, pe) in zip(halves, staged):
        n2 = _rms_normalize(x2).astype(MM_DTYPE)
        pg = _sigmoid(_dot(n2, wpg_ref[...]) + bpg_ref[...])
        o_ref[rows, :] = x2 + pg * pe


def _out_proj(x, p, ya, yr, gates, woa, wout, wpg, bpg, wple):
    B, S, _ = x.shape
    tm = TM_OUT
    row = lambda w: pl.BlockSpec((None, tm, w), lambda b, i: (b, i, 0))
    return pl.pallas_call(
        _out_kernel,
        grid=(B, S // tm),
        in_specs=[
            row(D_MODEL), row(PLE_DIM), row(ATT_W), row(D_MODEL), row(2 * D_MODEL),
            _const_spec(woa.shape), _const_spec(wout.shape), _const_spec(wpg.shape),
            _const_spec(bpg.shape), _const_spec(wple.shape),
        ],
        out_specs=row(D_MODEL),
        out_shape=jax.ShapeDtypeStruct((B, S, D_MODEL), x.dtype),
        compiler_params=_params(2),
        name="out_proj",
    )(x, p, ya, yr, gates, woa, wout, wpg, bpg, wple)


def _rope_tables(s):
    ang = (np.arange(s, dtype=np.float64)[:, None]
           * ROPE_THETA ** (-np.arange(0, HEAD_DIM, 2, dtype=np.float64) / HEAD_DIM)[None, :])
    cos, sin = np.cos(ang), np.sin(ang)
    cos2 = np.concatenate([cos, cos], axis=1).astype(np.float32)
    sin2 = np.concatenate([-sin, sin], axis=1).astype(np.float32)
    tabs = []
    for _, dil in ATTN_PATTERNS:
        if dil == 1:
            tabs.append((jnp.asarray(cos2), jnp.asarray(sin2)))
        else:
            perm = lambda t: jnp.asarray(
                np.ascontiguousarray(t.reshape(s // dil, dil, HEAD_DIM).transpose(1, 0, 2)))
            tabs.append((perm(cos2), perm(sin2)))
    return tabs


def kernel(x, p, norm_mix, w_in, b_in, conv_w, conv_b, w_rg_a, b_rg_a, w_rg_x, b_rg_x,
           lru_lambda, q_norm, k_norm, w_o_rnn, w_o_att, w_out, norm_ple, w_ple_gate,
           b_ple_gate, w_ple):
    depth = w_in.shape[0]
    s = x.shape[1]
    tabs = _rope_tables(s)
    f32 = jnp.float32
    for layer in range(depth):
        w = w_in[layer] * norm_mix[layer].astype(f32)[:, None]
        b = b_in[layer].astype(f32)[None, :]
        cols = lambda a, lo, hi: a[:, lo:hi]
        grp = lambda a, g: cols(a, OFF_QKV + g * GROUP_W, OFF_QKV + (g + 1) * GROUP_W)
        wb = lambda lo, hi: (cols(w, lo, hi).astype(MM_DTYPE), cols(b, lo, hi))
        gq = q_norm[layer].astype(f32) * (HEAD_DIM ** -0.5 * LOG2_E)
        gk = k_norm[layer].astype(f32)
        attn_w = (grp(w, 0).astype(MM_DTYPE), grp(b, 0),
                  *wb(OFF_Z_ATT, OFF_GATES), *wb(OFF_GATES, w.shape[1]),
                  grp(w, 1).astype(MM_DTYPE), grp(b, 1),
                  grp(w, 2).astype(MM_DTYPE), grp(b, 2), gq, gk)
        wg = jnp.concatenate([w_rg_a[layer], w_rg_x[layer]], axis=2).astype(MM_DTYPE)
        bg = jnp.stack([b_rg_a[layer], b_rg_x[layer]], axis=0).astype(f32)
        rnn_w = (*wb(0, OFF_Z_RNN), *wb(OFF_Z_RNN, OFF_QKV), conv_w[layer].astype(f32),
                 conv_b[layer].astype(f32)[None, :], wg, bg,
                 lru_lambda[layer].astype(f32)[None, :], w_o_rnn[layer].astype(MM_DTYPE))
        qkv0, qkv1, qkv2, sz, gates = _in_proj_attn(x, attn_w, tabs)
        yr = _rnn_branch(x, rnn_w)
        ya = _attention(qkv0, qkv1, qkv2, sz)
        x = _out_proj(
            x, p[layer], ya, yr, gates,
            w_o_att[layer].astype(MM_DTYPE), w_out[layer].astype(MM_DTYPE),
            (w_ple_gate[layer] * norm_ple[layer].astype(f32)[:, None]).astype(MM_DTYPE),
            b_ple_gate[layer].astype(f32)[None, :], w_ple[layer].astype(MM_DTYPE))
    return x
```

```python
import functools
import math

import jax
import jax.numpy as jnp
import numpy as np
from jax import lax
from jax.experimental import pallas as pl
from jax.experimental.pallas import tpu as pltpu

D_MODEL = 1024
PLE_DIM = 256
D_RNN = 1280
RNN_BLOCKS = 10
RNN_BLOCK_W = D_RNN // RNN_BLOCKS
CONV_W = 4
LRU_C = 8.0
HEAD_DIM = 128
HEADS = 4
ATTN_PATTERNS = ((128, 1), (512, 4), (2048, 16))
N_GROUPS = len(ATTN_PATTERNS)
ATT_W = HEADS * HEAD_DIM
GROUP_W = 3 * ATT_W
ATT_BLK = 128
ROPE_THETA = 10000.0
EPS = 1e-6

OFF_Z_RNN = D_RNN
OFF_QKV = 2 * D_RNN
OFF_Z_ATT = OFF_QKV + N_GROUPS * GROUP_W
OFF_GATES = OFF_Z_ATT + ATT_W

LANES = 128
SUBLANES = 8
VMEM_LIMIT_BYTES = 56 * 1024 * 1024

TM_IN = 512
IN_SUB = 2
TT_RNN = 64
RNN_SUB_T = 32
TM_OUT = 1024
OUT_SUB = 4
COMBINE_ROWS = 256

MM_DTYPE = jnp.bfloat16
LOG2_E = math.log2(math.e)
NEG_INF = float("-inf")


def _sigmoid(v):
    return 1.0 / (1.0 + jnp.exp2(v * (-LOG2_E)))


def _rms_normalize(v):
    var = jnp.mean(v * v, axis=-1, keepdims=True)
    return v * lax.rsqrt(var + EPS)


def _dot(a, b):
    return jnp.dot(a, b, preferred_element_type=jnp.float32)


def _const_spec(shape):
    zeros = (0,) * len(shape)
    return pl.BlockSpec(shape, lambda *_: zeros, pipeline_mode=pl.Buffered(1))


def _params(n_grid):
    return pltpu.CompilerParams(
        dimension_semantics=("arbitrary",) * n_grid,
        vmem_limit_bytes=VMEM_LIMIT_BYTES,
    )


def _qk_epilogue(acc, gq, gk, cos2, sin2):
    outs = []
    for part, gain in ((0, gq), (1, gk)):
        for h in range(HEADS):
            lo = part * ATT_W + h * HEAD_DIM
            t = _rms_normalize(acc[:, lo:lo + HEAD_DIM]) * gain
            outs.append(t * cos2 + pltpu.roll(t, HEAD_DIM // 2, 1) * sin2)
    outs.append(acc[:, 2 * ATT_W:])
    return jnp.concatenate(outs, axis=1)


def _attn_natural(hb, rows, wq0_ref, bq0_ref, wz_ref, bz_ref, wgt_ref, bgt_ref, gq_ref, gk_ref,
                  cos0_ref, sin0_ref, qkv0_ref, sz_ref, gates_ref):
    acc = _dot(hb, wq0_ref[...]) + bq0_ref[...]
    qkv0_ref[rows, :] = _qk_epilogue(acc, gq_ref[0:1], gk_ref[0:1], cos0_ref[rows, :],
                                     sin0_ref[rows, :]).astype(qkv0_ref.dtype)
    z = _dot(hb, wz_ref[...]) + bz_ref[...]
    sz_ref[rows, :] = (z * _sigmoid(z)).astype(sz_ref.dtype)
    g = _dot(hb, wgt_ref[...]) + bgt_ref[...]
    gates_ref[rows, :] = _sigmoid(g).astype(gates_ref.dtype)


def _attn_dilated(hn, sub, w1_ref, b1_ref, w2_ref, b2_ref, gq_ref, gk_ref, cos1_ref, sin1_ref,
                  cos2_ref, sin2_ref, qkv1_ref, qkv2_ref, hs_ref):
    tm = hn.shape[0]
    n_slabs = D_MODEL // LANES
    for j in range(n_slabs):
        hs_ref[sub, j] = hn[:, j * LANES:(j + 1) * LANES]
    for (dil, w_ref, b_ref, gi, cos_ref, sin_ref, out_ref) in (
            (ATTN_PATTERNS[1][1], w1_ref, b1_ref, 1, cos1_ref, sin1_ref, qkv1_ref),
            (ATTN_PATTERNS[2][1], w2_ref, b2_ref, 2, cos2_ref, sin2_ref, qkv2_ref)):
        per = tm // dil
        classes = []
        for c in range(dil):
            classes.append(jnp.concatenate(
                [hs_ref[sub, j, pl.ds(c, per, stride=dil), :] for j in range(n_slabs)],
                axis=1))
        hp = jnp.concatenate(classes, axis=0).astype(MM_DTYPE)
        acc = _dot(hp, w_ref[...]) + b_ref[...]
        part = pl.ds(sub * per, per)
        cos2 = cos_ref[:, part, :].reshape(tm, HEAD_DIM)
        sin2 = sin_ref[:, part, :].reshape(tm, HEAD_DIM)
        res = _qk_epilogue(acc, gq_ref[gi:gi + 1], gk_ref[gi:gi + 1], cos2, sin2)
        out_ref[:, part, :] = res.astype(out_ref.dtype).reshape(dil, per, GROUP_W)


def _in_proj_attn_kernel(x_ref,
                         wq0_ref, bq0_ref, wz_ref, bz_ref, wgt_ref, bgt_ref, w1_ref, b1_ref,
                         w2_ref, b2_ref, gq_ref, gk_ref,
                         cos0_ref, sin0_ref, cos1_ref, sin1_ref, cos2_ref, sin2_ref,
                         qkv0_ref, qkv1_ref, qkv2_ref, sz_ref, gates_ref, hs_ref):
    ts = x_ref.shape[0] // IN_SUB
    for sub in range(IN_SUB):
        rows = pl.ds(sub * ts, ts)
        hn = _rms_normalize(x_ref[rows, :])
        _attn_natural(hn.astype(MM_DTYPE), rows, wq0_ref, bq0_ref, wz_ref, bz_ref, wgt_ref,
                      bgt_ref, gq_ref, gk_ref, cos0_ref, sin0_ref, qkv0_ref, sz_ref, gates_ref)
        _attn_dilated(hn, sub, w1_ref, b1_ref, w2_ref, b2_ref, gq_ref, gk_ref, cos1_ref,
                      sin1_ref, cos2_ref, sin2_ref, qkv1_ref, qkv2_ref, hs_ref)


def _in_proj_attn(x, attn_w, tabs):
    B, S, _ = x.shape
    tm = TM_IN
    d1, d2 = ATTN_PATTERNS[1][1], ATTN_PATTERNS[2][1]
    (cos0, sin0), (cos1, sin1), (cos2, sin2) = tabs
    row = lambda w: pl.BlockSpec((None, tm, w), lambda b, i: (b, i, 0))
    tab_specs = [
        pl.BlockSpec((tm, HEAD_DIM), lambda b, i: (i, 0)),
        pl.BlockSpec((tm, HEAD_DIM), lambda b, i: (i, 0)),
        pl.BlockSpec((d1, tm // d1, HEAD_DIM), lambda b, i: (0, i, 0)),
        pl.BlockSpec((d1, tm // d1, HEAD_DIM), lambda b, i: (0, i, 0)),
        pl.BlockSpec((d2, tm // d2, HEAD_DIM), lambda b, i: (0, i, 0)),
        pl.BlockSpec((d2, tm // d2, HEAD_DIM), lambda b, i: (0, i, 0)),
    ]
    in_specs = [row(D_MODEL)] + [_const_spec(a.shape) for a in attn_w] + tab_specs
    out_shape = (
        jax.ShapeDtypeStruct((B, S, GROUP_W), MM_DTYPE),
        jax.ShapeDtypeStruct((B, d1, S // d1, GROUP_W), MM_DTYPE),
        jax.ShapeDtypeStruct((B, d2, S // d2, GROUP_W), MM_DTYPE),
        jax.ShapeDtypeStruct((B, S, ATT_W), MM_DTYPE),
        jax.ShapeDtypeStruct((B, S, 2 * D_MODEL), MM_DTYPE),
    )
    out_specs = (
        row(GROUP_W),
        pl.BlockSpec((None, d1, tm // d1, GROUP_W), lambda b, i: (b, 0, i, 0)),
        pl.BlockSpec((None, d2, tm // d2, GROUP_W), lambda b, i: (b, 0, i, 0)),
        row(ATT_W),
        row(2 * D_MODEL),
    )
    return pl.pallas_call(
        _in_proj_attn_kernel,
        grid=(B, S // tm),
        in_specs=in_specs,
        out_specs=out_specs,
        out_shape=out_shape,
        scratch_shapes=[pltpu.VMEM((IN_SUB, D_MODEL // LANES, tm // IN_SUB, LANES),
                                   jnp.float32)],
        compiler_params=_params(2),
        name="in_proj_attn",
    )(x, *attn_w, cos0, sin0, cos1, sin1, cos2, sin2)


def _rnn_kernel(x_ref, perm_ref, wx_ref, bx_ref, wz_ref, bz_ref, cw_ref, cb_ref, wg_ref, bg_ref,
                lam_ref, wo_ref, yr_ref, xs_ref, h_ref):
    nb, tt, _ = x_ref.shape
    ts = RNN_SUB_T
    n_sub = tt // ts
    rows = nb * ts
    hist = (CONV_W - 1) * nb
    step = pl.program_id(0)

    @pl.when(step == 0)
    def _():
        xs_ref[0:hist, :] = jnp.zeros((hist, D_RNN), jnp.float32)
        h_ref[...] = jnp.zeros_like(h_ref)

    neg_lam = -lam_ref[...]
    softplus = jnp.maximum(neg_lam, 0.0) + jnp.log(1.0 + jnp.exp(-jnp.abs(neg_lam)))
    log2_a_scale = (-LRU_C * LOG2_E) * softplus

    def front(s):
        hn = jnp.concatenate(
            [_rms_normalize(x_ref[b, s * ts:(s + 1) * ts, :]) for b in range(nb)], axis=0)
        hp = _dot(perm_ref[0], hn.astype(MM_DTYPE)).astype(MM_DTYPE)
        xr = _dot(hp, wx_ref[...]) + bx_ref[...]
        z = _dot(hp, wz_ref[...]) + bz_ref[...]
        xs_ref[pl.ds(hist + s * rows, rows), :] = xr
        xc = cb_ref[...]
        for k in range(CONV_W):
            xc = xc + cw_ref[k:k + 1, :] * xs_ref[pl.ds(s * rows + k * nb, rows), :]
        xcb = xc.astype(MM_DTYPE)
        pre = [_dot(xcb[:, n * RNN_BLOCK_W:(n + 1) * RNN_BLOCK_W], wg_ref[n])
               for n in range(RNN_BLOCKS)]
        return xc, pre, z

    def back(s, staged, h):
        xc, pre, z = staged
        r = _sigmoid(jnp.concatenate([t[:, :RNN_BLOCK_W] for t in pre], axis=1)
                     + bg_ref[0:1, :])
        gi = _sigmoid(jnp.concatenate([t[:, RNN_BLOCK_W:] for t in pre], axis=1)
                      + bg_ref[1:2, :])
        a = jnp.exp2(log2_a_scale * r)
        y1 = 1.0 - a * a
        mult = jnp.where(y1 > 0.0, y1 * lax.rsqrt(y1), 0.0)
        gx = gi * xc
        hs = []
        for t in range(ts):
            sl = slice(t * nb, (t + 1) * nb)
            m_t = mult[sl]
            if s == 0 and t == 0:
                m_t = jnp.where(step == 0, 1.0, m_t)
            h = a[sl] * h + m_t * gx[sl]
            hs.append(h)
        y = (jnp.concatenate(hs, axis=0) * (z * _sigmoid(z))).astype(MM_DTYPE)
        yr = _dot(y, wo_ref[...])
        yn = _dot(perm_ref[1], yr.astype(yr_ref.dtype)).astype(yr_ref.dtype)
        for b in range(nb):
            yr_ref[b, s * ts:(s + 1) * ts, :] = yn[b * ts:(b + 1) * ts]
        return h

    h = h_ref[...]
    staged = {0: front(0)}
    for s in range(n_sub):
        if s + 1 < n_sub:
            staged[s + 1] = front(s + 1)
        h = back(s, staged.pop(s), h)
    h_ref[...] = h
    xs_ref[0:hist, :] = xs_ref[pl.ds(n_sub * rows, hist), :]


def _time_major_perms(nb, ts):
    r = np.arange(nb * ts)
    p = np.zeros((nb * ts, nb * ts), np.float32)
    p[r, (r % nb) * ts + r // nb] = 1.0
    return jnp.asarray(np.stack([p, p.T]), MM_DTYPE)


def _rnn_branch(x, rnn_w):
    B, S, _ = x.shape
    tt = TT_RNN
    perms = _time_major_perms(B, RNN_SUB_T)
    return pl.pallas_call(
        _rnn_kernel,
        grid=(S // tt,),
        in_specs=([pl.BlockSpec((B, tt, D_MODEL), lambda i: (0, i, 0)),
                   _const_spec(perms.shape)]
                  + [_const_spec(a.shape) for a in rnn_w]),
        out_specs=pl.BlockSpec((B, tt, D_MODEL), lambda i: (0, i, 0)),
        out_shape=jax.ShapeDtypeStruct((B, S, D_MODEL), MM_DTYPE),
        scratch_shapes=[
            pltpu.VMEM(((CONV_W - 1) * B + B * tt, D_RNN), jnp.float32),
            pltpu.VMEM((B, D_RNN), jnp.float32),
        ],
        compiler_params=_params(1),
        name="rnn_branch",
    )(x, perms, *rnn_w)


def _attn_kernel(q0, k0, v0, q1, k1, v1, q2, k2, v2, sz_ref, y_ref, o_nat, l_nat):
    S = sz_ref.shape[0]
    n_blk = S // ATT_BLK
    blk3 = (n_blk, ATT_BLK, HEAD_DIM)
    qi = lax.broadcasted_iota(jnp.int32, (1, ATT_BLK, ATT_BLK), 1)
    kj = lax.broadcasted_iota(jnp.int32, (1, ATT_BLK, ATT_BLK), 2)
    cur_ok = kj <= qi
    blk_id = lax.broadcasted_iota(jnp.int32, (n_blk, 1, 1), 0)
    qk_dims = (((2,), (2,)), ((0,), (0,)))
    pv_dims = (((2,), (1,)), ((0,), (0,)))

    for g, (q_ref, k_ref, v_ref) in enumerate(((q0, k0, v0), (q1, k1, v1), (q2, k2, v2))):
        dil = ATTN_PATTERNS[g][1]
        blocks_per_class = n_blk // dil
        q = q_ref[...].reshape(blk3)
        k = k_ref[...].reshape(blk3)
        v = v_ref[...].reshape(blk3)
        v1 = jnp.concatenate([v, jnp.ones_like(v)], axis=2)
        if blocks_per_class > 1:
            shift = lambda t: jnp.concatenate([t[:1], t[:-1]], axis=0)
            kk = jnp.concatenate([shift(k), k], axis=1)
            vv = jnp.concatenate([shift(v1), v1], axis=1)
            s = lax.dot_general(q, kk, qk_dims, preferred_element_type=jnp.float32)
            first = (blk_id & (blocks_per_class - 1)) == 0
            prev_ok = kj >= qi + jnp.where(first, ATT_BLK, 0)
            s_prev = jnp.where(prev_ok, s[:, :, :ATT_BLK], NEG_INF)
            s_cur = jnp.where(cur_ok, s[:, :, ATT_BLK:], NEG_INF)
            m = jnp.max(jnp.maximum(s_prev, s_cur), axis=-1, keepdims=True)
            e = jnp.concatenate([jnp.exp2(s_prev - m), jnp.exp2(s_cur - m)], axis=2)
        else:
            vv = v1
            s = lax.dot_general(q, k, qk_dims, preferred_element_type=jnp.float32)
            s = jnp.where(cur_ok, s, NEG_INF)
            m = jnp.max(s, axis=-1, keepdims=True)
            e = jnp.exp2(s - m)
        od = lax.dot_general(e.astype(MM_DTYPE), vv, pv_dims,
                             preferred_element_type=jnp.float32)
        den = od[:, :, HEAD_DIM:]
        o = od[:, :, :HEAD_DIM] * (1.0 / den)
        lse = m + jnp.log2(den)
        if dil == 1:
            o_nat[g] = o.reshape(S, HEAD_DIM)
            l_nat[g] = lse.reshape(S, HEAD_DIM)
        else:
            for nb in range(n_blk):
                c, m0 = divmod(nb, blocks_per_class)
                idx = pl.ds(m0 * ATT_BLK * dil + c, ATT_BLK, stride=dil)
                o_nat[g, idx, :] = o[nb]
                l_nat[g, idx, :] = lse[nb]

    def merge(i, carry):
        rows = pl.ds(pl.multiple_of(i * COMBINE_ROWS, COMBINE_ROWS), COMBINE_ROWS)
        ls = [l_nat[g, rows, :] for g in range(N_GROUPS)]
        m = functools.reduce(jnp.maximum, ls)
        ws = [jnp.exp2(l - m) for l in ls]
        den = functools.reduce(lambda p, q: p + q, ws)
        att = functools.reduce(
            lambda p, q: p + q, [w * o_nat[g, rows, :] for g, w in enumerate(ws)])
        att = att * (1.0 / den)
        y_ref[rows, :] = (att * sz_ref[rows, :].astype(jnp.float32)).astype(y_ref.dtype)
        return carry
    lax.fori_loop(0, S // COMBINE_ROWS, merge, 0)


def _attention(qkv0, qkv1, qkv2, sz):
    B, S, _ = qkv0.shape
    qkv1 = qkv1.reshape(B, S, GROUP_W)
    qkv2 = qkv2.reshape(B, S, GROUP_W)
    in_specs = []
    for _ in range(N_GROUPS):
        for part in range(3):
            in_specs.append(pl.BlockSpec((None, S, HEAD_DIM),
                                         lambda b, h, part=part: (b, 0, part * HEADS + h)))
    in_specs.append(pl.BlockSpec((None, S, HEAD_DIM), lambda b, h: (b, 0, h)))
    return pl.pallas_call(
        _attn_kernel,
        grid=(B, HEADS),
        in_specs=in_specs,
        out_specs=pl.BlockSpec((None, S, HEAD_DIM), lambda b, h: (b, 0, h)),
        out_shape=jax.ShapeDtypeStruct((B, S, ATT_W), MM_DTYPE),
        scratch_shapes=[
            pltpu.VMEM((N_GROUPS, S, HEAD_DIM), jnp.float32),
            pltpu.VMEM((N_GROUPS, S, HEAD_DIM), jnp.float32),
        ],
        compiler_params=_params(2),
        name="attention",
    )(qkv0, qkv0, qkv0, qkv1, qkv1, qkv1, qkv2, qkv2, qkv2, sz)


def _out_kernel(x_ref, p_ref, ya_ref, yr_ref, g_ref, woa_ref, wout_ref, wpg_ref, bpg_ref,
                wple_ref, o_ref):
    tm = x_ref.shape[0]
    halves = [pl.ds(i * (tm // OUT_SUB), tm // OUT_SUB) for i in range(OUT_SUB)]
    staged = []
    for rows in halves:
        pe = _dot(p_ref[rows, :].astype(MM_DTYPE), wple_ref[...])
        ya = _dot(ya_ref[rows, :], woa_ref[...])
        g0 = g_ref[rows, :D_MODEL].astype(jnp.float32)
        g1 = g_ref[rows, D_MODEL:].astype(jnp.float32)
        merged = g0 * yr_ref[rows, :].astype(jnp.float32) + g1 * ya
        x2 = x_ref[rows, :] + _dot(merged.astype(MM_DTYPE), wout_ref[...])
        staged.append((x2, pe))
    for rows, (x2, pe) in zip(halves, staged):
        n2 = _rms_normalize(x2).astype(MM_DTYPE)
        pg = _sigmoid(_dot(n2, wpg_ref[...]) + bpg_ref[...])
        o_ref[rows, :] = x2 + pg * pe


def _out_proj(x, p, ya, yr, gates, woa, wout, wpg, bpg, wple):
    B, S, _ = x.shape
    tm = TM_OUT
    row = lambda w: pl.BlockSpec((None, tm, w), lambda b, i: (b, i, 0))
    return pl.pallas_call(
        _out_kernel,
        grid=(B, S // tm),
        in_specs=[
            row(D_MODEL), row(PLE_DIM), row(ATT_W), row(D_MODEL), row(2 * D_MODEL),
            _const_spec(woa.shape), _const_spec(wout.shape), _const_spec(wpg.shape),
            _const_spec(bpg.shape), _const_spec(wple.shape),
        ],
        out_specs=row(D_MODEL),
        out_shape=jax.ShapeDtypeStruct((B, S, D_MODEL), x.dtype),
        compiler_params=_params(2),
        name="out_proj",
    )(x, p, ya, yr, gates, woa, wout, wpg, bpg, wple)


def _rope_tables(s):
    ang = (np.arange(s, dtype=np.float64)[:, None]
           * ROPE_THETA ** (-np.arange(0, HEAD_DIM, 2, dtype=np.float64) / HEAD_DIM)[None, :])
    cos, sin = np.cos(ang), np.sin(ang)
    cos2 = np.concatenate([cos, cos], axis=1).astype(np.float32)
    sin2 = np.concatenate([-sin, sin], axis=1).astype(np.float32)
    tabs = []
    for _, dil in ATTN_PATTERNS:
        if dil == 1:
            tabs.append((jnp.asarray(cos2), jnp.asarray(sin2)))
        else:
            perm = lambda t: jnp.asarray(
                np.ascontiguousarray(t.reshape(s // dil, dil, HEAD_DIM).transpose(1, 0, 2)))
            tabs.append((perm(cos2), perm(sin2)))
    return tabs


def kernel(x, p, norm_mix, w_in, b_in, conv_w, conv_b, w_rg_a, b_rg_a, w_rg_x, b_rg_x,
           lru_lambda, q_norm, k_norm, w_o_rnn, w_o_att, w_out, norm_ple, w_ple_gate,
           b_ple_gate, w_ple):
    depth = w_in.shape[0]
    s = x.shape[1]
    tabs = _rope_tables(s)
    f32 = jnp.float32
    for layer in range(depth):
        w = w_in[layer] * norm_mix[layer].astype(f32)[:, None]
        b = b_in[layer].astype(f32)[None, :]
        cols = lambda a, lo, hi: a[:, lo:hi]
        grp = lambda a, g: cols(a, OFF_QKV + g * GROUP_W, OFF_QKV + (g + 1) * GROUP_W)
        wb = lambda lo, hi: (cols(w, lo, hi).astype(MM_DTYPE), cols(b, lo, hi))
        gq = q_norm[layer].astype(f32) * (HEAD_DIM ** -0.5 * LOG2_E)
        gk = k_norm[layer].astype(f32)
        attn_w = (grp(w, 0).astype(MM_DTYPE), grp(b, 0),
                  *wb(OFF_Z_ATT, OFF_GATES), *wb(OFF_GATES, w.shape[1]),
                  grp(w, 1).astype(MM_DTYPE), grp(b, 1),
                  grp(w, 2).astype(MM_DTYPE), grp(b, 2), gq, gk)
        wg = jnp.concatenate([w_rg_a[layer], w_rg_x[layer]], axis=2).astype(MM_DTYPE)
        bg = jnp.stack([b_rg_a[layer], b_rg_x[layer]], axis=0).astype(f32)
        rnn_w = (*wb(0, OFF_Z_RNN), *wb(OFF_Z_RNN, OFF_QKV), conv_w[layer].astype(f32),
                 conv_b[layer].astype(f32)[None, :], wg, bg,
                 lru_lambda[layer].astype(f32)[None, :], w_o_rnn[layer].astype(MM_DTYPE))
        qkv0, qkv1, qkv2, sz, gates = _in_proj_attn(x, attn_w, tabs)
        yr = _rnn_branch(x, rnn_w)
        ya = _attention(qkv0, qkv1, qkv2, sz)
        x = _out_proj(
            x, p[layer], ya, yr, gates,
            w_o_att[layer].astype(MM_DTYPE), w_out[layer].astype(MM_DTYPE),
            (w_ple_gate[layer] * norm_ple[layer].astype(f32)[:, None]).astype(MM_DTYPE),
            b_ple_gate[layer].astype(f32)[None, :], w_ple[layer].astype(MM_DTYPE))
    return x
```

```python
import functools
import math

import jax
import jax.numpy as jnp
import numpy as np
from jax import lax
from jax.experimental import pallas as pl
from jax.experimental.pallas import tpu as pltpu

D_MODEL = 1024
PLE_DIM = 256
D_RNN = 1280
RNN_BLOCKS = 10
RNN_BLOCK_W = D_RNN // RNN_BLOCKS
CONV_W = 4
LRU_C = 8.0
HEAD_DIM = 128
HEADS = 4
ATTN_PATTERNS = ((128, 1), (512, 4), (2048, 16))
N_GROUPS = len(ATTN_PATTERNS)
ATT_W = HEADS * HEAD_DIM
GROUP_W = 3 * ATT_W
ATT_BLK = 128
ROPE_THETA = 10000.0
EPS = 1e-6

OFF_Z_RNN = D_RNN
OFF_QKV = 2 * D_RNN
OFF_Z_ATT = OFF_QKV + N_GROUPS * GROUP_W
OFF_GATES = OFF_Z_ATT + ATT_W

LANES = 128
SUBLANES = 8
VMEM_LIMIT_BYTES = 56 * 1024 * 1024

TM_IN = 512
IN_SUB = 2
TT_RNN = 64
RNN_SUB_T = 16
TM_OUT = 1024
OUT_SUB = 4
COMBINE_ROWS = 256
ATT_HEADS_PER_STEP = 2

MM_DTYPE = jnp.bfloat16
LOG2_E = math.log2(math.e)
NEG_INF = float("-inf")


def _sigmoid(v):
    return 1.0 / (1.0 + jnp.exp2(v * (-LOG2_E)))


def _rms_normalize(v):
    var = jnp.mean(v * v, axis=-1, keepdims=True)
    return v * lax.rsqrt(var + EPS)


def _dot(a, b):
    return jnp.dot(a, b, preferred_element_type=jnp.float32)


def _const_spec(shape):
    zeros = (0,) * len(shape)
    return pl.BlockSpec(shape, lambda *_: zeros, pipeline_mode=pl.Buffered(1))


def _params(n_grid):
    return pltpu.CompilerParams(
        dimension_semantics=("arbitrary",) * n_grid,
        vmem_limit_bytes=VMEM_LIMIT_BYTES,
    )


def _qk_epilogue(acc, gq, gk, cos2, sin2):
    outs = []
    for part, gain in ((0, gq), (1, gk)):
        for h in range(HEADS):
            lo = part * ATT_W + h * HEAD_DIM
            t = _rms_normalize(acc[:, lo:lo + HEAD_DIM]) * gain
            outs.append(t * cos2 + pltpu.roll(t, HEAD_DIM // 2, 1) * sin2)
    outs.append(acc[:, 2 * ATT_W:])
    return jnp.concatenate(outs, axis=1)


def _attn_natural(hb, rows, wq0_ref, bq0_ref, wz_ref, bz_ref, wgt_ref, bgt_ref, gq_ref, gk_ref,
                  cos0_ref, sin0_ref, qkv0_ref, sz_ref, gates_ref):
    acc = _dot(hb, wq0_ref[...]) + bq0_ref[...]
    qkv0_ref[rows, :] = _qk_epilogue(acc, gq_ref[0:1], gk_ref[0:1], cos0_ref[rows, :],
                                     sin0_ref[rows, :]).astype(qkv0_ref.dtype)
    z = _dot(hb, wz_ref[...]) + bz_ref[...]
    sz_ref[rows, :] = (z * _sigmoid(z)).astype(sz_ref.dtype)
    g = _dot(hb, wgt_ref[...]) + bgt_ref[...]
    gates_ref[rows, :] = _sigmoid(g).astype(gates_ref.dtype)


def _attn_dilated(hn, sub, w1_ref, b1_ref, w2_ref, b2_ref, gq_ref, gk_ref, cos1_ref, sin1_ref,
                  cos2_ref, sin2_ref, qkv1_ref, qkv2_ref, hs_ref):
    tm = hn.shape[0]
    n_slabs = D_MODEL // LANES
    for j in range(n_slabs):
        hs_ref[sub, j] = hn[:, j * LANES:(j + 1) * LANES]
    for (dil, w_ref, b_ref, gi, cos_ref, sin_ref, out_ref) in (
            (ATTN_PATTERNS[1][1], w1_ref, b1_ref, 1, cos1_ref, sin1_ref, qkv1_ref),
            (ATTN_PATTERNS[2][1], w2_ref, b2_ref, 2, cos2_ref, sin2_ref, qkv2_ref)):
        per = tm // dil
        classes = []
        for c in range(dil):
            classes.append(jnp.concatenate(
                [hs_ref[sub, j, pl.ds(c, per, stride=dil), :] for j in range(n_slabs)],
                axis=1))
        hp = jnp.concatenate(classes, axis=0).astype(MM_DTYPE)
        acc = _dot(hp, w_ref[...]) + b_ref[...]
        part = pl.ds(sub * per, per)
        cos2 = cos_ref[:, part, :].reshape(tm, HEAD_DIM)
        sin2 = sin_ref[:, part, :].reshape(tm, HEAD_DIM)
        res = _qk_epilogue(acc, gq_ref[gi:gi + 1], gk_ref[gi:gi + 1], cos2, sin2)
        out_ref[:, part, :] = res.astype(out_ref.dtype).reshape(dil, per, GROUP_W)


def _in_proj_attn_kernel(x_ref,
                         wq0_ref, bq0_ref, wz_ref, bz_ref, wgt_ref, bgt_ref, w1_ref, b1_ref,
                         w2_ref, b2_ref, gq_ref, gk_ref,
                         cos0_ref, sin0_ref, cos1_ref, sin1_ref, cos2_ref, sin2_ref,
                         qkv0_ref, qkv1_ref, qkv2_ref, sz_ref, gates_ref, hs_ref):
    ts = x_ref.shape[0] // IN_SUB
    for sub in range(IN_SUB):
        rows = pl.ds(sub * ts, ts)
        hn = _rms_normalize(x_ref[rows, :])
        _attn_natural(hn.astype(MM_DTYPE), rows, wq0_ref, bq0_ref, wz_ref, bz_ref, wgt_ref,
                      bgt_ref, gq_ref, gk_ref, cos0_ref, sin0_ref, qkv0_ref, sz_ref, gates_ref)
        _attn_dilated(hn, sub, w1_ref, b1_ref, w2_ref, b2_ref, gq_ref, gk_ref, cos1_ref,
                      sin1_ref, cos2_ref, sin2_ref, qkv1_ref, qkv2_ref, hs_ref)


def _in_proj_attn(x, attn_w, tabs):
    B, S, _ = x.shape
    tm = TM_IN
    d1, d2 = ATTN_PATTERNS[1][1], ATTN_PATTERNS[2][1]
    (cos0, sin0), (cos1, sin1), (cos2, sin2) = tabs
    row = lambda w: pl.BlockSpec((None, tm, w), lambda b, i: (b, i, 0))
    tab_specs = [
        pl.BlockSpec((tm, HEAD_DIM), lambda b, i: (i, 0)),
        pl.BlockSpec((tm, HEAD_DIM), lambda b, i: (i, 0)),
        pl.BlockSpec((d1, tm // d1, HEAD_DIM), lambda b, i: (0, i, 0)),
        pl.BlockSpec((d1, tm // d1, HEAD_DIM), lambda b, i: (0, i, 0)),
        pl.BlockSpec((d2, tm // d2, HEAD_DIM), lambda b, i: (0, i, 0)),
        pl.BlockSpec((d2, tm // d2, HEAD_DIM), lambda b, i: (0, i, 0)),
    ]
    in_specs = [row(D_MODEL)] + [_const_spec(a.shape) for a in attn_w] + tab_specs
    out_shape = (
        jax.ShapeDtypeStruct((B, S, GROUP_W), MM_DTYPE),
        jax.ShapeDtypeStruct((B, d1, S // d1, GROUP_W), MM_DTYPE),
        jax.ShapeDtypeStruct((B, d2, S // d2, GROUP_W), MM_DTYPE),
        jax.ShapeDtypeStruct((B, S, ATT_W), MM_DTYPE),
        jax.ShapeDtypeStruct((B, S, 2 * D_MODEL), MM_DTYPE),
    )
    out_specs = (
        row(GROUP_W),
        pl.BlockSpec((None, d1, tm // d1, GROUP_W), lambda b, i: (b, 0, i, 0)),
        pl.BlockSpec((None, d2, tm // d2, GROUP_W), lambda b, i: (b, 0, i, 0)),
        row(ATT_W),
        row(2 * D_MODEL),
    )
    return pl.pallas_call(
        _in_proj_attn_kernel,
        grid=(B, S // tm),
        in_specs=in_specs,
        out_specs=out_specs,
        out_shape=out_shape,
        scratch_shapes=[pltpu.VMEM((IN_SUB, D_MODEL // LANES, tm // IN_SUB, LANES),
                                   jnp.float32)],
        compiler_params=_params(2),
        name="in_proj_attn",
    )(x, *attn_w, cos0, sin0, cos1, sin1, cos2, sin2)


def _rnn_kernel(x_ref, perm_ref, wx_ref, bx_ref, wz_ref, bz_ref, cw_ref, cb_ref, wg_ref, bg_ref,
                lam_ref, wo_ref, yr_ref, xs_ref, h_ref):
    nb, tt, _ = x_ref.shape
    ts = RNN_SUB_T
    n_sub = tt // ts
    rows = nb * ts
    hist = (CONV_W - 1) * nb
    step = pl.program_id(0)

    @pl.when(step == 0)
    def _():
        xs_ref[0:hist, :] = jnp.zeros((hist, D_RNN), jnp.float32)
        h_ref[...] = jnp.zeros_like(h_ref)

    neg_lam = -lam_ref[...]
    softplus = jnp.maximum(neg_lam, 0.0) + jnp.log(1.0 + jnp.exp(-jnp.abs(neg_lam)))
    log2_a_scale = (-LRU_C * LOG2_E) * softplus

    def front(s):
        hn = jnp.concatenate(
            [_rms_normalize(x_ref[b, s * ts:(s + 1) * ts, :]) for b in range(nb)], axis=0)
        hp = _dot(perm_ref[0], hn.astype(MM_DTYPE)).astype(MM_DTYPE)
        xr = _dot(hp, wx_ref[...]) + bx_ref[...]
        z = _dot(hp, wz_ref[...]) + bz_ref[...]
        xs_ref[pl.ds(hist + s * rows, rows), :] = xr
        xc = cb_ref[...]
        for k in range(CONV_W):
            xc = xc + cw_ref[k:k + 1, :] * xs_ref[pl.ds(s * rows + k * nb, rows), :]
        xcb = xc.astype(MM_DTYPE)
        pre = [_dot(xcb[:, n * RNN_BLOCK_W:(n + 1) * RNN_BLOCK_W], wg_ref[n])
               for n in range(RNN_BLOCKS)]
        return xc, pre, z

    def back(s, staged, h):
        xc, pre, z = staged
        r = _sigmoid(jnp.concatenate([t[:, :RNN_BLOCK_W] for t in pre], axis=1)
                     + bg_ref[0:1, :])
        gi = _sigmoid(jnp.concatenate([t[:, RNN_BLOCK_W:] for t in pre], axis=1)
                      + bg_ref[1:2, :])
        a = jnp.exp2(log2_a_scale * r)
        y1 = 1.0 - a * a
        mult = jnp.where(y1 > 0.0, y1 * lax.rsqrt(y1), 0.0)
        gx = gi * xc
        hs = []
        for t in range(ts):
            sl = slice(t * nb, (t + 1) * nb)
            m_t = mult[sl]
            if s == 0 and t == 0:
                m_t = jnp.where(step == 0, 1.0, m_t)
            h = a[sl] * h + m_t * gx[sl]
            hs.append(h)
        y = (jnp.concatenate(hs, axis=0) * (z * _sigmoid(z))).astype(MM_DTYPE)
        yr = _dot(y, wo_ref[...])
        yn = _dot(perm_ref[1], yr.astype(yr_ref.dtype)).astype(yr_ref.dtype)
        for b in range(nb):
            yr_ref[b, s * ts:(s + 1) * ts, :] = yn[b * ts:(b + 1) * ts]
        return h

    h = h_ref[...]
    staged = {0: front(0)}
    for s in range(n_sub):
        if s + 1 < n_sub:
            staged[s + 1] = front(s + 1)
        h = back(s, staged.pop(s), h)
    h_ref[...] = h
    xs_ref[0:hist, :] = xs_ref[pl.ds(n_sub * rows, hist), :]


def _time_major_perms(nb, ts):
    r = np.arange(nb * ts)
    p = np.zeros((nb * ts, nb * ts), np.float32)
    p[r, (r % nb) * ts + r // nb] = 1.0
    return jnp.asarray(np.stack([p, p.T]), MM_DTYPE)


def _rnn_branch(x, rnn_w):
    B, S, _ = x.shape
    tt = TT_RNN
    perms = _time_major_perms(B, RNN_SUB_T)
    return pl.pallas_call(
        _rnn_kernel,
        grid=(S // tt,),
        in_specs=([pl.BlockSpec((B, tt, D_MODEL), lambda i: (0, i, 0)),
                   _const_spec(perms.shape)]
                  + [_const_spec(a.shape) for a in rnn_w]),
        out_specs=pl.BlockSpec((B, tt, D_MODEL), lambda i: (0, i, 0)),
        out_shape=jax.ShapeDtypeStruct((B, S, D_MODEL), MM_DTYPE),
        scratch_shapes=[
            pltpu.VMEM(((CONV_W - 1) * B + B * tt, D_RNN), jnp.float32),
            pltpu.VMEM((B, D_RNN), jnp.float32),
        ],
        compiler_params=_params(1),
        name="rnn_branch",
    )(x, perms, *rnn_w)


def _attn_head(hd, q0, k0, v0, q1, k1, v1, q2, k2, v2, o_nat, l_nat):
    S = q0.shape[0]
    n_blk = S // ATT_BLK
    blk3 = (n_blk, ATT_BLK, HEAD_DIM)
    qi = lax.broadcasted_iota(jnp.int32, (1, ATT_BLK, ATT_BLK), 1)
    kj = lax.broadcasted_iota(jnp.int32, (1, ATT_BLK, ATT_BLK), 2)
    cur_ok = kj <= qi
    blk_id = lax.broadcasted_iota(jnp.int32, (n_blk, 1, 1), 0)
    qk_dims = (((2,), (2,)), ((0,), (0,)))
    pv_dims = (((2,), (1,)), ((0,), (0,)))

    for g, (q_ref, k_ref, v_ref) in enumerate(((q0, k0, v0), (q1, k1, v1), (q2, k2, v2))):
        dil = ATTN_PATTERNS[g][1]
        blocks_per_class = n_blk // dil
        q = q_ref[:, hd].reshape(blk3)
        k = k_ref[:, hd].reshape(blk3)
        v = v_ref[:, hd].reshape(blk3)
        v1 = jnp.concatenate([v, jnp.ones_like(v)], axis=2)
        if blocks_per_class > 1:
            shift = lambda t: jnp.concatenate([t[:1], t[:-1]], axis=0)
            kk = jnp.concatenate([shift(k), k], axis=1)
            vv = jnp.concatenate([shift(v1), v1], axis=1)
            s = lax.dot_general(q, kk, qk_dims, preferred_element_type=jnp.float32)
            first = (blk_id & (blocks_per_class - 1)) == 0
            prev_ok = kj >= qi + jnp.where(first, ATT_BLK, 0)
            s_prev = jnp.where(prev_ok, s[:, :, :ATT_BLK], NEG_INF)
            s_cur = jnp.where(cur_ok, s[:, :, ATT_BLK:], NEG_INF)
            m = jnp.max(jnp.maximum(s_prev, s_cur), axis=-1, keepdims=True)
            e = jnp.concatenate([jnp.exp2(s_prev - m), jnp.exp2(s_cur - m)], axis=2)
        else:
            vv = v1
            s = lax.dot_general(q, k, qk_dims, preferred_element_type=jnp.float32)
            s = jnp.where(cur_ok, s, NEG_INF)
            m = jnp.max(s, axis=-1, keepdims=True)
            e = jnp.exp2(s - m)
        od = lax.dot_general(e.astype(MM_DTYPE), vv, pv_dims,
                             preferred_element_type=jnp.float32)
        den = od[:, :, HEAD_DIM:]
        o = od[:, :, :HEAD_DIM] * (1.0 / den)
        lse = m + jnp.log2(den)
        if dil == 1:
            o_nat[g] = o.reshape(S, HEAD_DIM)
            l_nat[g] = lse.reshape(S, HEAD_DIM)
        else:
            for nb in range(n_blk):
                c, m0 = divmod(nb, blocks_per_class)
                idx = pl.ds(m0 * ATT_BLK * dil + c, ATT_BLK, stride=dil)
                o_nat[g, idx, :] = o[nb]
                l_nat[g, idx, :] = lse[nb]


def _attn_kernel(q0, k0, v0, q1, k1, v1, q2, k2, v2, sz_ref, y_ref, o_nat, l_nat):
    S = sz_ref.shape[0]
    for i in range(ATT_HEADS_PER_STEP):
        hd = slice(i * HEAD_DIM, (i + 1) * HEAD_DIM)
        _attn_head(hd, q0, k0, v0, q1, k1, v1, q2, k2, v2, o_nat.at[i], l_nat.at[i])

    def merge(i, carry):
        rows = pl.ds(pl.multiple_of(i * COMBINE_ROWS, COMBINE_ROWS), COMBINE_ROWS)
        for j in range(ATT_HEADS_PER_STEP):
            hd = slice(j * HEAD_DIM, (j + 1) * HEAD_DIM)
            ls = [l_nat[j, g, rows, :] for g in range(N_GROUPS)]
            m = functools.reduce(jnp.maximum, ls)
            ws = [jnp.exp2(l - m) for l in ls]
            den = functools.reduce(lambda p, q: p + q, ws)
            att = functools.reduce(
                lambda p, q: p + q, [w * o_nat[j, g, rows, :] for g, w in enumerate(ws)])
            att = att * (1.0 / den)
            y_ref[rows, hd] = (att * sz_ref[rows, hd].astype(jnp.float32)).astype(y_ref.dtype)
        return carry
    lax.fori_loop(0, S // COMBINE_ROWS, merge, 0)


def _attention(qkv0, qkv1, qkv2, sz):
    B, S, _ = qkv0.shape
    hps = ATT_HEADS_PER_STEP
    w = hps * HEAD_DIM
    qkv1 = qkv1.reshape(B, S, GROUP_W)
    qkv2 = qkv2.reshape(B, S, GROUP_W)
    in_specs = []
    for _ in range(N_GROUPS):
        for part in range(3):
            in_specs.append(pl.BlockSpec(
                (None, S, w), lambda b, h, part=part: (b, 0, part * (HEADS // hps) + h)))
    in_specs.append(pl.BlockSpec((None, S, w), lambda b, h: (b, 0, h)))
    return pl.pallas_call(
        _attn_kernel,
        grid=(B, HEADS // hps),
        in_specs=in_specs,
        out_specs=pl.BlockSpec((None, S, w), lambda b, h: (b, 0, h)),
        out_shape=jax.ShapeDtypeStruct((B, S, ATT_W), MM_DTYPE),
        scratch_shapes=[
            pltpu.VMEM((hps, N_GROUPS, S, HEAD_DIM), jnp.float32),
            pltpu.VMEM((hps, N_GROUPS, S, HEAD_DIM), jnp.float32),
        ],
        compiler_params=_params(2),
        name="attention",
    )(qkv0, qkv0, qkv0, qkv1, qkv1, qkv1, qkv2, qkv2, qkv2, sz)


def _out_kernel(x_ref, p_ref, ya_ref, yr_ref, g_ref, woa_ref, wout_ref, wpg_ref, bpg_ref,
                wple_ref, o_ref):
    tm = x_ref.shape[0]
    halves = [pl.ds(i * (tm // OUT_SUB), tm // OUT_SUB) for i in range(OUT_SUB)]
    staged = []
    for rows in halves:
        pe = _dot(p_ref[rows, :].astype(MM_DTYPE), wple_ref[...])
        ya = _dot(ya_ref[rows, :], woa_ref[...])
        g0 = g_ref[rows, :D_MODEL].astype(jnp.float32)
        g1 = g_ref[rows, D_MODEL:].astype(jnp.float32)
        merged = g0 * yr_ref[rows, :].astype(jnp.float32) + g1 * ya
        x2 = x_ref[rows, :] + _dot(merged.astype(MM_DTYPE), wout_ref[...])
        staged.append((x2, pe))
    for rows, (x2, pe) in zip(halves, staged):
        n2 = _rms_normalize(x2).astype(MM_DTYPE)
        pg = _sigmoid(_dot(n2, wpg_ref[...]) + bpg_ref[...])
        o_ref[rows, :] = x2 + pg * pe


def _out_proj(x, p, ya, yr, gates, woa, wout, wpg, bpg, wple):
    B, S, _ = x.shape
    tm = TM_OUT
    row = lambda w: pl.BlockSpec((None, tm, w), lambda b, i: (b, i, 0))
    return pl.pallas_call(
        _out_kernel,
        grid=(B, S // tm),
        in_specs=[
            row(D_MODEL), row(PLE_DIM), row(ATT_W), row(D_MODEL), row(2 * D_MODEL),
            _const_spec(woa.shape), _const_spec(wout.shape), _const_spec(wpg.shape),
            _const_spec(bpg.shape), _const_spec(wple.shape),
        ],
        out_specs=row(D_MODEL),
        out_shape=jax.ShapeDtypeStruct((B, S, D_MODEL), x.dtype),
        compiler_params=_params(2),
        name="out_proj",
    )(x, p, ya, yr, gates, woa, wout, wpg, bpg, wple)


def _rope_tables(s):
    ang = (np.arange(s, dtype=np.float64)[:, None]
           * ROPE_THETA ** (-np.arange(0, HEAD_DIM, 2, dtype=np.float64) / HEAD_DIM)[None, :])
    cos, sin = np.cos(ang), np.sin(ang)
    cos2 = np.concatenate([cos, cos], axis=1).astype(np.float32)
    sin2 = np.concatenate([-sin, sin], axis=1).astype(np.float32)
    tabs = []
    for _, dil in ATTN_PATTERNS:
        if dil == 1:
            tabs.append((jnp.asarray(cos2), jnp.asarray(sin2)))
        else:
            perm = lambda t: jnp.asarray(
                np.ascontiguousarray(t.reshape(s // dil, dil, HEAD_DIM).transpose(1, 0, 2)))
            tabs.append((perm(cos2), perm(sin2)))
    return tabs


def kernel(x, p, norm_mix, w_in, b_in, conv_w, conv_b, w_rg_a, b_rg_a, w_rg_x, b_rg_x,
           lru_lambda, q_norm, k_norm, w_o_rnn, w_o_att, w_out, norm_ple, w_ple_gate,
           b_ple_gate, w_ple):
    depth = w_in.shape[0]
    s = x.shape[1]
    tabs = _rope_tables(s)
    f32 = jnp.float32
    for layer in range(depth):
        w = w_in[layer] * norm_mix[layer].astype(f32)[:, None]
        b = b_in[layer].astype(f32)[None, :]
        cols = lambda a, lo, hi: a[:, lo:hi]
        grp = lambda a, g: cols(a, OFF_QKV + g * GROUP_W, OFF_QKV + (g + 1) * GROUP_W)
        wb = lambda lo, hi: (cols(w, lo, hi).astype(MM_DTYPE), cols(b, lo, hi))
        gq = q_norm[layer].astype(f32) * (HEAD_DIM ** -0.5 * LOG2_E)
        gk = k_norm[layer].astype(f32)
        attn_w = (grp(w, 0).astype(MM_DTYPE), grp(b, 0),
                  *wb(OFF_Z_ATT, OFF_GATES), *wb(OFF_GATES, w.shape[1]),
                  grp(w, 1).astype(MM_DTYPE), grp(b, 1),
                  grp(w, 2).astype(MM_DTYPE), grp(b, 2), gq, gk)
        wg = jnp.concatenate([w_rg_a[layer], w_rg_x[layer]], axis=2).astype(MM_DTYPE)
        bg = jnp.stack([b_rg_a[layer], b_rg_x[layer]], axis=0).astype(f32)
        rnn_w = (*wb(0, OFF_Z_RNN), *wb(OFF_Z_RNN, OFF_QKV), conv_w[layer].astype(f32),
                 conv_b[layer].astype(f32)[None, :], wg, bg,
                 lru_lambda[layer].astype(f32)[None, :], w_o_rnn[layer].astype(MM_DTYPE))
        qkv0, qkv1, qkv2, sz, gates = _in_proj_attn(x, attn_w, tabs)
        yr = _rnn_branch(x, rnn_w)
        ya = _attention(qkv0, qkv1, qkv2, sz)
        x = _out_proj(
            x, p[layer], ya, yr, gates,
            w_o_att[layer].astype(MM_DTYPE), w_out[layer].astype(MM_DTYPE),
            (w_ple_gate[layer] * norm_ple[layer].astype(f32)[:, None]).astype(MM_DTYPE),
            b_ple_gate[layer].astype(f32)[None, :], w_ple[layer].astype(MM_DTYPE))
    return x
```

```python
import functools
import math

import jax
import jax.numpy as jnp
import numpy as np
from jax import lax
from jax.experimental import pallas as pl
from jax.experimental.pallas import tpu as pltpu

D_MODEL = 1024
PLE_DIM = 256
D_RNN = 1280
RNN_BLOCKS = 10
RNN_BLOCK_W = D_RNN // RNN_BLOCKS
CONV_W = 4
LRU_C = 8.0
HEAD_DIM = 128
HEADS = 4
ATTN_PATTERNS = ((128, 1), (512, 4), (2048, 16))
N_GROUPS = len(ATTN_PATTERNS)
ATT_W = HEADS * HEAD_DIM
GROUP_W = 3 * ATT_W
ATT_BLK = 128
ROPE_THETA = 10000.0
EPS = 1e-6

OFF_Z_RNN = D_RNN
OFF_QKV = 2 * D_RNN
OFF_Z_ATT = OFF_QKV + N_GROUPS * GROUP_W
OFF_GATES = OFF_Z_ATT + ATT_W

LANES = 128
SUBLANES = 8
VMEM_LIMIT_BYTES = 56 * 1024 * 1024

TM_IN = 512
IN_SUB = 2
TT_RNN = 64
RNN_SUB_T = 16
TM_OUT = 1024
OUT_SUB = 4
COMBINE_ROWS = 256
ATT_HEADS_PER_STEP = 2

MM_DTYPE = jnp.bfloat16
LOG2_E = math.log2(math.e)
NEG_INF = float("-inf")


def _sigmoid(v):
    return 1.0 / (1.0 + jnp.exp2(v * (-LOG2_E)))


def _rms_normalize(v):
    var = jnp.mean(v * v, axis=-1, keepdims=True)
    return v * lax.rsqrt(var + EPS)


def _dot(a, b):
    return jnp.dot(a, b, preferred_element_type=jnp.float32)


def _const_spec(shape):
    zeros = (0,) * len(shape)
    return pl.BlockSpec(shape, lambda *_: zeros, pipeline_mode=pl.Buffered(1))


def _params(n_grid):
    return pltpu.CompilerParams(
        dimension_semantics=("arbitrary",) * n_grid,
        vmem_limit_bytes=VMEM_LIMIT_BYTES,
    )


def _qk_epilogue(acc, gq, gk, cos2, sin2):
    outs = []
    for part, gain in ((0, gq), (1, gk)):
        for h in range(HEADS):
            lo = part * ATT_W + h * HEAD_DIM
            t = _rms_normalize(acc[:, lo:lo + HEAD_DIM]) * gain
            outs.append(t * cos2 + pltpu.roll(t, HEAD_DIM // 2, 1) * sin2)
    outs.append(acc[:, 2 * ATT_W:])
    return jnp.concatenate(outs, axis=1)


def _attn_natural(hb, rows, wq0_ref, bq0_ref, wz_ref, bz_ref, wgt_ref, bgt_ref, gq_ref, gk_ref,
                  cos0_ref, sin0_ref, qkv0_ref, sz_ref, gates_ref):
    acc = _dot(hb, wq0_ref[...]) + bq0_ref[...]
    qkv0_ref[rows, :] = _qk_epilogue(acc, gq_ref[0:1], gk_ref[0:1], cos0_ref[rows, :],
                                     sin0_ref[rows, :]).astype(qkv0_ref.dtype)
    z = _dot(hb, wz_ref[...]) + bz_ref[...]
    sz_ref[rows, :] = (z * _sigmoid(z)).astype(sz_ref.dtype)
    g = _dot(hb, wgt_ref[...]) + bgt_ref[...]
    gates_ref[rows, :] = _sigmoid(g).astype(gates_ref.dtype)


def _attn_dilated(hn, sub, w1_ref, b1_ref, w2_ref, b2_ref, gq_ref, gk_ref, cos1_ref, sin1_ref,
                  cos2_ref, sin2_ref, qkv1_ref, qkv2_ref, hs_ref):
    tm = hn.shape[0]
    n_slabs = D_MODEL // LANES
    for j in range(n_slabs):
        hs_ref[sub, j] = hn[:, j * LANES:(j + 1) * LANES]
    for (dil, w_ref, b_ref, gi, cos_ref, sin_ref, out_ref) in (
            (ATTN_PATTERNS[1][1], w1_ref, b1_ref, 1, cos1_ref, sin1_ref, qkv1_ref),
            (ATTN_PATTERNS[2][1], w2_ref, b2_ref, 2, cos2_ref, sin2_ref, qkv2_ref)):
        per = tm // dil
        classes = []
        for c in range(dil):
            classes.append(jnp.concatenate(
                [hs_ref[sub, j, pl.ds(c, per, stride=dil), :] for j in range(n_slabs)],
                axis=1))
        hp = jnp.concatenate(classes, axis=0).astype(MM_DTYPE)
        acc = _dot(hp, w_ref[...]) + b_ref[...]
        part = pl.ds(sub * per, per)
        cos2 = cos_ref[:, part, :].reshape(tm, HEAD_DIM)
        sin2 = sin_ref[:, part, :].reshape(tm, HEAD_DIM)
        res = _qk_epilogue(acc, gq_ref[gi:gi + 1], gk_ref[gi:gi + 1], cos2, sin2)
        out_ref[:, part, :] = res.astype(out_ref.dtype).reshape(dil, per, GROUP_W)


def _in_proj_attn_kernel(x_ref,
                         wq0_ref, bq0_ref, wz_ref, bz_ref, wgt_ref, bgt_ref, w1_ref, b1_ref,
                         w2_ref, b2_ref, gq_ref, gk_ref,
                         cos0_ref, sin0_ref, cos1_ref, sin1_ref, cos2_ref, sin2_ref,
                         qkv0_ref, qkv1_ref, qkv2_ref, sz_ref, gates_ref, hb_ref, hs_ref):
    ts = x_ref.shape[0] // IN_SUB
    for sub in range(IN_SUB):
        rows = pl.ds(sub * ts, ts)
        hn = _rms_normalize(x_ref[rows, :])
        hb = hn.astype(MM_DTYPE)
        hb_ref[rows, :] = hb
        _attn_natural(hb, rows, wq0_ref, bq0_ref, wz_ref, bz_ref, wgt_ref,
                      bgt_ref, gq_ref, gk_ref, cos0_ref, sin0_ref, qkv0_ref, sz_ref, gates_ref)
        _attn_dilated(hn, sub, w1_ref, b1_ref, w2_ref, b2_ref, gq_ref, gk_ref, cos1_ref,
                      sin1_ref, cos2_ref, sin2_ref, qkv1_ref, qkv2_ref, hs_ref)


def _in_proj_attn(x, attn_w, tabs):
    B, S, _ = x.shape
    tm = TM_IN
    d1, d2 = ATTN_PATTERNS[1][1], ATTN_PATTERNS[2][1]
    (cos0, sin0), (cos1, sin1), (cos2, sin2) = tabs
    row = lambda w: pl.BlockSpec((None, tm, w), lambda b, i: (b, i, 0))
    tab_specs = [
        pl.BlockSpec((tm, HEAD_DIM), lambda b, i: (i, 0)),
        pl.BlockSpec((tm, HEAD_DIM), lambda b, i: (i, 0)),
        pl.BlockSpec((d1, tm // d1, HEAD_DIM), lambda b, i: (0, i, 0)),
        pl.BlockSpec((d1, tm // d1, HEAD_DIM), lambda b, i: (0, i, 0)),
        pl.BlockSpec((d2, tm // d2, HEAD_DIM), lambda b, i: (0, i, 0)),
        pl.BlockSpec((d2, tm // d2, HEAD_DIM), lambda b, i: (0, i, 0)),
    ]
    in_specs = [row(D_MODEL)] + [_const_spec(a.shape) for a in attn_w] + tab_specs
    out_shape = (
        jax.ShapeDtypeStruct((B, S, GROUP_W), MM_DTYPE),
        jax.ShapeDtypeStruct((B, d1, S // d1, GROUP_W), MM_DTYPE),
        jax.ShapeDtypeStruct((B, d2, S // d2, GROUP_W), MM_DTYPE),
        jax.ShapeDtypeStruct((B, S, ATT_W), MM_DTYPE),
        jax.ShapeDtypeStruct((B, S, 2 * D_MODEL), MM_DTYPE),
        jax.ShapeDtypeStruct((B, S, D_MODEL), MM_DTYPE),
    )
    out_specs = (
        row(GROUP_W),
        pl.BlockSpec((None, d1, tm // d1, GROUP_W), lambda b, i: (b, 0, i, 0)),
        pl.BlockSpec((None, d2, tm // d2, GROUP_W), lambda b, i: (b, 0, i, 0)),
        row(ATT_W),
        row(2 * D_MODEL),
        row(D_MODEL),
    )
    return pl.pallas_call(
        _in_proj_attn_kernel,
        grid=(B, S // tm),
        in_specs=in_specs,
        out_specs=out_specs,
        out_shape=out_shape,
        scratch_shapes=[pltpu.VMEM((IN_SUB, D_MODEL // LANES, tm // IN_SUB, LANES),
                                   jnp.float32)],
        compiler_params=_params(2),
        name="in_proj_attn",
    )(x, *attn_w, cos0, sin0, cos1, sin1, cos2, sin2)


def _rnn_kernel(hb_ref, perm_ref, wx_ref, bx_ref, wz_ref, bz_ref, cw_ref, cb_ref, wg_ref, bg_ref,
                lam_ref, wo_ref, yr_ref, xs_ref, h_ref):
    nb, tt, _ = hb_ref.shape
    ts = RNN_SUB_T
    n_sub = tt // ts
    rows = nb * ts
    hist = (CONV_W - 1) * nb
    step = pl.program_id(0)

    @pl.when(step == 0)
    def _():
        xs_ref[0:hist, :] = jnp.broadcast_to(-bx_ref[...], (hist, D_RNN))
        h_ref[...] = jnp.zeros_like(h_ref)

    conv_bias = cb_ref[...] + bx_ref[...] * functools.reduce(
        lambda p, q: p + q, [cw_ref[k:k + 1, :] for k in range(CONV_W)])
    neg_lam = -lam_ref[...]
    softplus = jnp.maximum(neg_lam, 0.0) + jnp.log(1.0 + jnp.exp(-jnp.abs(neg_lam)))
    log2_a_scale = (-LRU_C * LOG2_E) * softplus

    def front(s):
        hb = jnp.concatenate([hb_ref[b, s * ts:(s + 1) * ts, :] for b in range(nb)], axis=0)
        hp = _dot(perm_ref[0], hb).astype(MM_DTYPE)
        xs_ref[pl.ds(hist + s * rows, rows), :] = _dot(hp, wx_ref[...])
        z = _dot(hp, wz_ref[...]) + bz_ref[...]
        xc = conv_bias
        for k in range(CONV_W):
            xc = xc + cw_ref[k:k + 1, :] * xs_ref[pl.ds(s * rows + k * nb, rows), :]
        xcb = xc.astype(MM_DTYPE)
        pre = [_dot(xcb[:, n * RNN_BLOCK_W:(n + 1) * RNN_BLOCK_W], wg_ref[n])
               for n in range(RNN_BLOCKS)]
        return xc, pre, z

    def back(s, staged, h):
        xc, pre, z = staged
        r = _sigmoid(jnp.concatenate([t[:, :RNN_BLOCK_W] for t in pre], axis=1)
                     + bg_ref[0:1, :])
        gi = _sigmoid(jnp.concatenate([t[:, RNN_BLOCK_W:] for t in pre], axis=1)
                      + bg_ref[1:2, :])
        a = jnp.exp2(log2_a_scale * r)
        y1 = 1.0 - a * a
        mult = jnp.where(y1 > 0.0, y1 * lax.rsqrt(y1), 0.0)
        gx = gi * xc
        hs = []
        for t in range(ts):
            sl = slice(t * nb, (t + 1) * nb)
            m_t = mult[sl]
            if s == 0 and t == 0:
                m_t = jnp.where(step == 0, 1.0, m_t)
            h = a[sl] * h + m_t * gx[sl]
            hs.append(h)
        y = (jnp.concatenate(hs, axis=0) * (z * _sigmoid(z))).astype(MM_DTYPE)
        yr = _dot(y, wo_ref[...])
        yn = _dot(perm_ref[1], yr.astype(yr_ref.dtype)).astype(yr_ref.dtype)
        for b in range(nb):
            yr_ref[b, s * ts:(s + 1) * ts, :] = yn[b * ts:(b + 1) * ts]
        return h

    h = h_ref[...]
    staged = {0: front(0)}
    for s in range(n_sub):
        if s + 1 < n_sub:
            staged[s + 1] = front(s + 1)
        h = back(s, staged.pop(s), h)
    h_ref[...] = h
    xs_ref[0:hist, :] = xs_ref[pl.ds(n_sub * rows, hist), :]


def _time_major_perms(nb, ts):
    r = np.arange(nb * ts)
    p = np.zeros((nb * ts, nb * ts), np.float32)
    p[r, (r % nb) * ts + r // nb] = 1.0
    return jnp.asarray(np.stack([p, p.T]), MM_DTYPE)


def _rnn_branch(hb, rnn_w):
    B, S, _ = hb.shape
    tt = TT_RNN
    perms = _time_major_perms(B, RNN_SUB_T)
    return pl.pallas_call(
        _rnn_kernel,
        grid=(S // tt,),
        in_specs=([pl.BlockSpec((B, tt, D_MODEL), lambda i: (0, i, 0)),
                   _const_spec(perms.shape)]
                  + [_const_spec(a.shape) for a in rnn_w]),
        out_specs=pl.BlockSpec((B, tt, D_MODEL), lambda i: (0, i, 0)),
        out_shape=jax.ShapeDtypeStruct((B, S, D_MODEL), MM_DTYPE),
        scratch_shapes=[
            pltpu.VMEM(((CONV_W - 1) * B + B * tt, D_RNN), jnp.float32),
            pltpu.VMEM((B, D_RNN), jnp.float32),
        ],
        compiler_params=_params(1),
        name="rnn_branch",
    )(hb, perms, *rnn_w)


def _attn_head(hd, q0, k0, v0, q1, k1, v1, q2, k2, v2, o_nat, l_nat):
    S = q0.shape[0]
    n_blk = S // ATT_BLK
    blk3 = (n_blk, ATT_BLK, HEAD_DIM)
    qi = lax.broadcasted_iota(jnp.int32, (1, ATT_BLK, ATT_BLK), 1)
    kj = lax.broadcasted_iota(jnp.int32, (1, ATT_BLK, ATT_BLK), 2)
    cur_ok = kj <= qi
    blk_id = lax.broadcasted_iota(jnp.int32, (n_blk, 1, 1), 0)
    qk_dims = (((2,), (2,)), ((0,), (0,)))
    pv_dims = (((2,), (1,)), ((0,), (0,)))

    for g, (q_ref, k_ref, v_ref) in enumerate(((q0, k0, v0), (q1, k1, v1), (q2, k2, v2))):
        dil = ATTN_PATTERNS[g][1]
        blocks_per_class = n_blk // dil
        q = q_ref[:, hd].reshape(blk3)
        k = k_ref[:, hd].reshape(blk3)
        v = v_ref[:, hd].reshape(blk3)
        v1 = jnp.concatenate([v, jnp.ones_like(v)], axis=2)
        if blocks_per_class > 1:
            shift = lambda t: jnp.concatenate([t[:1], t[:-1]], axis=0)
            kk = jnp.concatenate([shift(k), k], axis=1)
            vv = jnp.concatenate([shift(v1), v1], axis=1)
            s = lax.dot_general(q, kk, qk_dims, preferred_element_type=jnp.float32)
            first = (blk_id & (blocks_per_class - 1)) == 0
            prev_ok = kj >= qi + jnp.where(first, ATT_BLK, 0)
            s_prev = jnp.where(prev_ok, s[:, :, :ATT_BLK], NEG_INF)
            s_cur = jnp.where(cur_ok, s[:, :, ATT_BLK:], NEG_INF)
            m = jnp.max(jnp.maximum(s_prev, s_cur), axis=-1, keepdims=True)
            e = jnp.concatenate([jnp.exp2(s_prev - m), jnp.exp2(s_cur - m)], axis=2)
        else:
            vv = v1
            s = lax.dot_general(q, k, qk_dims, preferred_element_type=jnp.float32)
            s = jnp.where(cur_ok, s, NEG_INF)
            m = jnp.max(s, axis=-1, keepdims=True)
            e = jnp.exp2(s - m)
        od = lax.dot_general(e.astype(MM_DTYPE), vv, pv_dims,
                             preferred_element_type=jnp.float32)
        den = od[:, :, HEAD_DIM:]
        o = od[:, :, :HEAD_DIM] * (1.0 / den)
        lse = m + jnp.log2(den)
        if dil == 1:
            o_nat[g] = o.reshape(S, HEAD_DIM)
            l_nat[g] = lse.reshape(S, HEAD_DIM)
        else:
            for nb in range(n_blk):
                c, m0 = divmod(nb, blocks_per_class)
                idx = pl.ds(m0 * ATT_BLK * dil + c, ATT_BLK, stride=dil)
                o_nat[g, idx, :] = o[nb]
                l_nat[g, idx, :] = lse[nb]


def _attn_kernel(q0, k0, v0, q1, k1, v1, q2, k2, v2, sz_ref, y_ref, o_nat, l_nat):
    S = sz_ref.shape[0]
    for i in range(ATT_HEADS_PER_STEP):
        hd = slice(i * HEAD_DIM, (i + 1) * HEAD_DIM)
        _attn_head(hd, q0, k0, v0, q1, k1, v1, q2, k2, v2, o_nat.at[i], l_nat.at[i])

    def merge(i, carry):
        rows = pl.ds(pl.multiple_of(i * COMBINE_ROWS, COMBINE_ROWS), COMBINE_ROWS)
        for j in range(ATT_HEADS_PER_STEP):
            hd = slice(j * HEAD_DIM, (j + 1) * HEAD_DIM)
            ls = [l_nat[j, g, rows, :] for g in range(N_GROUPS)]
            m = functools.reduce(jnp.maximum, ls)
            ws = [jnp.exp2(l - m) for l in ls]
            den = functools.reduce(lambda p, q: p + q, ws)
            att = functools.reduce(
                lambda p, q: p + q, [w * o_nat[j, g, rows, :] for g, w in enumerate(ws)])
            att = att * (1.0 / den)
            y_ref[rows, hd] = (att * sz_ref[rows, hd].astype(jnp.float32)).astype(y_ref.dtype)
        return carry
    lax.fori_loop(0, S // COMBINE_ROWS, merge, 0)


def _attention(qkv0, qkv1, qkv2, sz):
    B, S, _ = qkv0.shape
    hps = ATT_HEADS_PER_STEP
    w = hps * HEAD_DIM
    qkv1 = qkv1.reshape(B, S, GROUP_W)
    qkv2 = qkv2.reshape(B, S, GROUP_W)
    in_specs = []
    for _ in range(N_GROUPS):
        for part in range(3):
            in_specs.append(pl.BlockSpec(
                (None, S, w), lambda b, h, part=part: (b, 0, part * (HEADS // hps) + h)))
    in_specs.append(pl.BlockSpec((None, S, w), lambda b, h: (b, 0, h)))
    return pl.pallas_call(
        _attn_kernel,
        grid=(B, HEADS // hps),
        in_specs=in_specs,
        out_specs=pl.BlockSpec((None, S, w), lambda b, h: (b, 0, h)),
        out_shape=jax.ShapeDtypeStruct((B, S, ATT_W), MM_DTYPE),
        scratch_shapes=[
            pltpu.VMEM((hps, N_GROUPS, S, HEAD_DIM), jnp.float32),
            pltpu.VMEM((hps, N_GROUPS, S, HEAD_DIM), jnp.float32),
        ],
        compiler_params=_params(2),
        name="attention",
    )(qkv0, qkv0, qkv0, qkv1, qkv1, qkv1, qkv2, qkv2, qkv2, sz)


def _out_kernel(x_ref, p_ref, ya_ref, yr_ref, g_ref, woa_ref, wout_ref, wpg_ref, bpg_ref,
                wple_ref, o_ref):
    tm = x_ref.shape[0]
    halves = [pl.ds(i * (tm // OUT_SUB), tm // OUT_SUB) for i in range(OUT_SUB)]
    staged = []
    for rows in halves:
        pe = _dot(p_ref[rows, :].astype(MM_DTYPE), wple_ref[...])
        ya = _dot(ya_ref[rows, :], woa_ref[...])
        g0 = g_ref[rows, :D_MODEL].astype(jnp.float32)
        g1 = g_ref[rows, D_MODEL:].astype(jnp.float32)
        merged = g0 * yr_ref[rows, :].astype(jnp.float32) + g1 * ya
        x2 = x_ref[rows, :] + _dot(merged.astype(MM_DTYPE), wout_ref[...])
        staged.append((x2, pe))
    for rows, (x2, pe) in zip(halves, staged):
        n2 = _rms_normalize(x2).astype(MM_DTYPE)
        pg = _sigmoid(_dot(n2, wpg_ref[...]) + bpg_ref[...])
        o_ref[rows, :] = x2 + pg * pe


def _out_proj(x, p, ya, yr, gates, woa, wout, wpg, bpg, wple):
    B, S, _ = x.shape
    tm = TM_OUT
    row = lambda w: pl.BlockSpec((None, tm, w), lambda b, i: (b, i, 0))
    return pl.pallas_call(
        _out_kernel,
        grid=(B, S // tm),
        in_specs=[
            row(D_MODEL), row(PLE_DIM), row(ATT_W), row(D_MODEL), row(2 * D_MODEL),
            _const_spec(woa.shape), _const_spec(wout.shape), _const_spec(wpg.shape),
            _const_spec(bpg.shape), _const_spec(wple.shape),
        ],
        out_specs=row(D_MODEL),
        out_shape=jax.ShapeDtypeStruct((B, S, D_MODEL), x.dtype),
        compiler_params=_params(2),
        name="out_proj",
    )(x, p, ya, yr, gates, woa, wout, wpg, bpg, wple)


def _rope_tables(s):
    ang = (np.arange(s, dtype=np.float64)[:, None]
           * ROPE_THETA ** (-np.arange(0, HEAD_DIM, 2, dtype=np.float64) / HEAD_DIM)[None, :])
    cos, sin = np.cos(ang), np.sin(ang)
    cos2 = np.concatenate([cos, cos], axis=1).astype(np.float32)
    sin2 = np.concatenate([-sin, sin], axis=1).astype(np.float32)
    tabs = []
    for _, dil in ATTN_PATTERNS:
        if dil == 1:
            tabs.append((jnp.asarray(cos2), jnp.asarray(sin2)))
        else:
            perm = lambda t: jnp.asarray(
                np.ascontiguousarray(t.reshape(s // dil, dil, HEAD_DIM).transpose(1, 0, 2)))
            tabs.append((perm(cos2), perm(sin2)))
    return tabs


def kernel(x, p, norm_mix, w_in, b_in, conv_w, conv_b, w_rg_a, b_rg_a, w_rg_x, b_rg_x,
           lru_lambda, q_norm, k_norm, w_o_rnn, w_o_att, w_out, norm_ple, w_ple_gate,
           b_ple_gate, w_ple):
    depth = w_in.shape[0]
    s = x.shape[1]
    tabs = _rope_tables(s)
    f32 = jnp.float32
    for layer in range(depth):
        w = w_in[layer] * norm_mix[layer].astype(f32)[:, None]
        b = b_in[layer].astype(f32)[None, :]
        cols = lambda a, lo, hi: a[:, lo:hi]
        grp = lambda a, g: cols(a, OFF_QKV + g * GROUP_W, OFF_QKV + (g + 1) * GROUP_W)
        wb = lambda lo, hi: (cols(w, lo, hi).astype(MM_DTYPE), cols(b, lo, hi))
        gq = q_norm[layer].astype(f32) * (HEAD_DIM ** -0.5 * LOG2_E)
        gk = k_norm[layer].astype(f32)
        attn_w = (grp(w, 0).astype(MM_DTYPE), grp(b, 0),
                  *wb(OFF_Z_ATT, OFF_GATES), *wb(OFF_GATES, w.shape[1]),
                  grp(w, 1).astype(MM_DTYPE), grp(b, 1),
                  grp(w, 2).astype(MM_DTYPE), grp(b, 2), gq, gk)
        wg = jnp.concatenate([w_rg_a[layer], w_rg_x[layer]], axis=2).astype(MM_DTYPE)
        bg = jnp.stack([b_rg_a[layer], b_rg_x[layer]], axis=0).astype(f32)
        rnn_w = (*wb(0, OFF_Z_RNN), *wb(OFF_Z_RNN, OFF_QKV), conv_w[layer].astype(f32),
                 conv_b[layer].astype(f32)[None, :], wg, bg,
                 lru_lambda[layer].astype(f32)[None, :], w_o_rnn[layer].astype(MM_DTYPE))
        qkv0, qkv1, qkv2, sz, gates, hb = _in_proj_attn(x, attn_w, tabs)
        yr = _rnn_branch(hb, rnn_w)
        ya = _attention(qkv0, qkv1, qkv2, sz)
        x = _out_proj(
            x, p[layer], ya, yr, gates,
            w_o_att[layer].astype(MM_DTYPE), w_out[layer].astype(MM_DTYPE),
            (w_ple_gate[layer] * norm_ple[layer].astype(f32)[:, None]).astype(MM_DTYPE),
            b_ple_gate[layer].astype(f32)[None, :], w_ple[layer].astype(MM_DTYPE))
    return x
```

```python
import functools
import math

import jax
import jax.numpy as jnp
import numpy as np
from jax import lax
from jax.experimental import pallas as pl
from jax.experimental.pallas import tpu as pltpu

D_MODEL = 1024
PLE_DIM = 256
D_RNN = 1280
RNN_BLOCKS = 10
RNN_BLOCK_W = D_RNN // RNN_BLOCKS
CONV_W = 4
LRU_C = 8.0
HEAD_DIM = 128
HEADS = 4
ATTN_PATTERNS = ((128, 1), (512, 4), (2048, 16))
N_GROUPS = len(ATTN_PATTERNS)
ATT_W = HEADS * HEAD_DIM
GROUP_W = 3 * ATT_W
ATT_BLK = 128
ROPE_THETA = 10000.0
EPS = 1e-6

OFF_Z_RNN = D_RNN
OFF_QKV = 2 * D_RNN
OFF_Z_ATT = OFF_QKV + N_GROUPS * GROUP_W
OFF_GATES = OFF_Z_ATT + ATT_W

LANES = 128
SUBLANES = 8
VMEM_LIMIT_BYTES = 56 * 1024 * 1024

TM_IN = 512
IN_SUB = 2
TT_RNN = 64
RNN_SUB_T = 16
TM_OUT = 1024
OUT_SUB = 4
COMBINE_ROWS = 256
ATT_HEADS_PER_STEP = 2

MM_DTYPE = jnp.bfloat16
LOG2_E = math.log2(math.e)
EXPM1_SERIES_BELOW = 2.0 ** -11
EXPM1_C1 = -2.0 * math.log(2.0)
EXPM1_C2 = -2.0 * math.log(2.0) ** 2
NEG_INF = float("-inf")


def _sigmoid(v):
    return 1.0 / (1.0 + jnp.exp2(v * (-LOG2_E)))


def _rms_normalize(v):
    var = jnp.mean(v * v, axis=-1, keepdims=True)
    return v * lax.rsqrt(var + EPS)


def _dot(a, b):
    return jnp.dot(a, b, preferred_element_type=jnp.float32)


def _const_spec(shape):
    zeros = (0,) * len(shape)
    return pl.BlockSpec(shape, lambda *_: zeros, pipeline_mode=pl.Buffered(1))


def _params(n_grid):
    return pltpu.CompilerParams(
        dimension_semantics=("arbitrary",) * n_grid,
        vmem_limit_bytes=VMEM_LIMIT_BYTES,
    )


def _qk_epilogue(acc, gq, gk, cos2, sin2):
    outs = []
    for part, gain in ((0, gq), (1, gk)):
        for h in range(HEADS):
            lo = part * ATT_W + h * HEAD_DIM
            t = _rms_normalize(acc[:, lo:lo + HEAD_DIM]) * gain
            outs.append(t * cos2 + pltpu.roll(t, HEAD_DIM // 2, 1) * sin2)
    outs.append(acc[:, 2 * ATT_W:])
    return jnp.concatenate(outs, axis=1)


def _attn_natural(hb, rows, wq0_ref, bq0_ref, wz_ref, bz_ref, wgt_ref, bgt_ref, gq_ref, gk_ref,
                  cos0_ref, sin0_ref, qkv0_ref, sz_ref, gates_ref):
    acc = _dot(hb, wq0_ref[...]) + bq0_ref[...]
    qkv0_ref[rows, :] = _qk_epilogue(acc, gq_ref[0:1], gk_ref[0:1], cos0_ref[rows, :],
                                     sin0_ref[rows, :]).astype(qkv0_ref.dtype)
    z = _dot(hb, wz_ref[...]) + bz_ref[...]
    sz_ref[rows, :] = (z * _sigmoid(z)).astype(sz_ref.dtype)
    g = _dot(hb, wgt_ref[...]) + bgt_ref[...]
    gates_ref[rows, :] = _sigmoid(g).astype(gates_ref.dtype)


def _attn_dilated(hn, sub, w1_ref, b1_ref, w2_ref, b2_ref, gq_ref, gk_ref, cos1_ref, sin1_ref,
                  cos2_ref, sin2_ref, qkv1_ref, qkv2_ref, hs_ref):
    tm = hn.shape[0]
    n_slabs = D_MODEL // LANES
    for j in range(n_slabs):
        hs_ref[sub, j] = hn[:, j * LANES:(j + 1) * LANES]
    for (dil, w_ref, b_ref, gi, cos_ref, sin_ref, out_ref) in (
            (ATTN_PATTERNS[1][1], w1_ref, b1_ref, 1, cos1_ref, sin1_ref, qkv1_ref),
            (ATTN_PATTERNS[2][1], w2_ref, b2_ref, 2, cos2_ref, sin2_ref, qkv2_ref)):
        per = tm // dil
        classes = []
        for c in range(dil):
            classes.append(jnp.concatenate(
                [hs_ref[sub, j, pl.ds(c, per, stride=dil), :] for j in range(n_slabs)],
                axis=1))
        hp = jnp.concatenate(classes, axis=0).astype(MM_DTYPE)
        acc = _dot(hp, w_ref[...]) + b_ref[...]
        part = pl.ds(sub * per, per)
        cos2 = cos_ref[:, part, :].reshape(tm, HEAD_DIM)
        sin2 = sin_ref[:, part, :].reshape(tm, HEAD_DIM)
        res = _qk_epilogue(acc, gq_ref[gi:gi + 1], gk_ref[gi:gi + 1], cos2, sin2)
        out_ref[:, part, :] = res.astype(out_ref.dtype).reshape(dil, per, GROUP_W)


def _in_proj_attn_kernel(x_ref,
                         wq0_ref, bq0_ref, wz_ref, bz_ref, wgt_ref, bgt_ref, w1_ref, b1_ref,
                         w2_ref, b2_ref, gq_ref, gk_ref,
                         cos0_ref, sin0_ref, cos1_ref, sin1_ref, cos2_ref, sin2_ref,
                         qkv0_ref, qkv1_ref, qkv2_ref, sz_ref, gates_ref, hs_ref):
    ts = x_ref.shape[0] // IN_SUB
    for sub in range(IN_SUB):
        rows = pl.ds(sub * ts, ts)
        hn = _rms_normalize(x_ref[rows, :])
        _attn_natural(hn.astype(MM_DTYPE), rows, wq0_ref, bq0_ref, wz_ref, bz_ref, wgt_ref,
                      bgt_ref, gq_ref, gk_ref, cos0_ref, sin0_ref, qkv0_ref, sz_ref, gates_ref)
        _attn_dilated(hn, sub, w1_ref, b1_ref, w2_ref, b2_ref, gq_ref, gk_ref, cos1_ref,
                      sin1_ref, cos2_ref, sin2_ref, qkv1_ref, qkv2_ref, hs_ref)


def _in_proj_attn(x, attn_w, tabs):
    B, S, _ = x.shape
    tm = TM_IN
    d1, d2 = ATTN_PATTERNS[1][1], ATTN_PATTERNS[2][1]
    (cos0, sin0), (cos1, sin1), (cos2, sin2) = tabs
    row = lambda w: pl.BlockSpec((None, tm, w), lambda b, i: (b, i, 0))
    tab_specs = [
        pl.BlockSpec((tm, HEAD_DIM), lambda b, i: (i, 0)),
        pl.BlockSpec((tm, HEAD_DIM), lambda b, i: (i, 0)),
        pl.BlockSpec((d1, tm // d1, HEAD_DIM), lambda b, i: (0, i, 0)),
        pl.BlockSpec((d1, tm // d1, HEAD_DIM), lambda b, i: (0, i, 0)),
        pl.BlockSpec((d2, tm // d2, HEAD_DIM), lambda b, i: (0, i, 0)),
        pl.BlockSpec((d2, tm // d2, HEAD_DIM), lambda b, i: (0, i, 0)),
    ]
    in_specs = [row(D_MODEL)] + [_const_spec(a.shape) for a in attn_w] + tab_specs
    out_shape = (
        jax.ShapeDtypeStruct((B, S, GROUP_W), MM_DTYPE),
        jax.ShapeDtypeStruct((B, d1, S // d1, GROUP_W), MM_DTYPE),
        jax.ShapeDtypeStruct((B, d2, S // d2, GROUP_W), MM_DTYPE),
        jax.ShapeDtypeStruct((B, S, ATT_W), MM_DTYPE),
        jax.ShapeDtypeStruct((B, S, 2 * D_MODEL), MM_DTYPE),
    )
    out_specs = (
        row(GROUP_W),
        pl.BlockSpec((None, d1, tm // d1, GROUP_W), lambda b, i: (b, 0, i, 0)),
        pl.BlockSpec((None, d2, tm // d2, GROUP_W), lambda b, i: (b, 0, i, 0)),
        row(ATT_W),
        row(2 * D_MODEL),
    )
    return pl.pallas_call(
        _in_proj_attn_kernel,
        grid=(B, S // tm),
        in_specs=in_specs,
        out_specs=out_specs,
        out_shape=out_shape,
        scratch_shapes=[pltpu.VMEM((IN_SUB, D_MODEL // LANES, tm // IN_SUB, LANES),
                                   jnp.float32)],
        compiler_params=_params(2),
        name="in_proj_attn",
    )(x, *attn_w, cos0, sin0, cos1, sin1, cos2, sin2)


def _rnn_kernel(x_ref, perm_ref, wx_ref, bx_ref, wz_ref, bz_ref, cw_ref, cb_ref, wg_ref, bg_ref,
                lam_ref, wo_ref, yr_ref, xs_ref, h_ref):
    nb, tt, _ = x_ref.shape
    ts = RNN_SUB_T
    n_sub = tt // ts
    rows = nb * ts
    hist = (CONV_W - 1) * nb
    step = pl.program_id(0)

    @pl.when(step == 0)
    def _():
        xs_ref[0:hist, :] = jnp.zeros((hist, D_RNN), jnp.float32)
        h_ref[...] = jnp.zeros_like(h_ref)

    neg_lam = -lam_ref[...]
    softplus = jnp.maximum(neg_lam, 0.0) + jnp.log1p(jnp.exp(-jnp.abs(neg_lam)))
    log2_a_scale = (-LRU_C * LOG2_E) * softplus

    def front(s):
        hn = jnp.concatenate(
            [_rms_normalize(x_ref[b, s * ts:(s + 1) * ts, :]) for b in range(nb)], axis=0)
        hp = _dot(perm_ref[0], hn.astype(MM_DTYPE)).astype(MM_DTYPE)
        xr = _dot(hp, wx_ref[...]) + bx_ref[...]
        z = _dot(hp, wz_ref[...]) + bz_ref[...]
        xs_ref[pl.ds(hist + s * rows, rows), :] = xr
        xc = cb_ref[...]
        for k in range(CONV_W):
            xc = xc + cw_ref[k:k + 1, :] * xs_ref[pl.ds(s * rows + k * nb, rows), :]
        xcb = xc.astype(MM_DTYPE)
        pre = [_dot(xcb[:, n * RNN_BLOCK_W:(n + 1) * RNN_BLOCK_W], wg_ref[n])
               for n in range(RNN_BLOCKS)]
        return xc, pre, z

    def back(s, staged, h):
        xc, pre, z = staged
        r = _sigmoid(jnp.concatenate([t[:, :RNN_BLOCK_W] for t in pre], axis=1)
                     + bg_ref[0:1, :])
        gi = _sigmoid(jnp.concatenate([t[:, RNN_BLOCK_W:] for t in pre], axis=1)
                      + bg_ref[1:2, :])
        t = log2_a_scale * r
        a = jnp.exp2(t)
        y1 = jnp.where(t > -EXPM1_SERIES_BELOW,
                       t * (EXPM1_C1 + EXPM1_C2 * t), 1.0 - a * a)
        mult = jnp.where(y1 > 0.0, y1 * lax.rsqrt(y1), 0.0)
        gx = gi * xc
        hs = []
        for t in range(ts):
            sl = slice(t * nb, (t + 1) * nb)
            m_t = mult[sl]
            if s == 0 and t == 0:
                m_t = jnp.where(step == 0, 1.0, m_t)
            h = a[sl] * h + m_t * gx[sl]
            hs.append(h)
        y = (jnp.concatenate(hs, axis=0) * (z * _sigmoid(z))).astype(MM_DTYPE)
        yr = _dot(y, wo_ref[...])
        yn = _dot(perm_ref[1], yr.astype(yr_ref.dtype)).astype(yr_ref.dtype)
        for b in range(nb):
            yr_ref[b, s * ts:(s + 1) * ts, :] = yn[b * ts:(b + 1) * ts]
        return h

    h = h_ref[...]
    staged = {0: front(0)}
    for s in range(n_sub):
        if s + 1 < n_sub:
            staged[s + 1] = front(s + 1)
        h = back(s, staged.pop(s), h)
    h_ref[...] = h
    xs_ref[0:hist, :] = xs_ref[pl.ds(n_sub * rows, hist), :]


def _time_major_perms(nb, ts):
    r = np.arange(nb * ts)
    p = np.zeros((nb * ts, nb * ts), np.float32)
    p[r, (r % nb) * ts + r // nb] = 1.0
    return jnp.asarray(np.stack([p, p.T]), MM_DTYPE)


def _rnn_branch(x, rnn_w):
    B, S, _ = x.shape
    tt = TT_RNN
    perms = _time_major_perms(B, RNN_SUB_T)
    return pl.pallas_call(
        _rnn_kernel,
        grid=(S // tt,),
        in_specs=([pl.BlockSpec((B, tt, D_MODEL), lambda i: (0, i, 0)),
                   _const_spec(perms.shape)]
                  + [_const_spec(a.shape) for a in rnn_w]),
        out_specs=pl.BlockSpec((B, tt, D_MODEL), lambda i: (0, i, 0)),
        out_shape=jax.ShapeDtypeStruct((B, S, D_MODEL), MM_DTYPE),
        scratch_shapes=[
            pltpu.VMEM(((CONV_W - 1) * B + B * tt, D_RNN), jnp.float32),
            pltpu.VMEM((B, D_RNN), jnp.float32),
        ],
        compiler_params=_params(1),
        name="rnn_branch",
    )(x, perms, *rnn_w)


def _attn_head(hd, q0, k0, v0, q1, k1, v1, q2, k2, v2, o_nat, l_nat):
    S = q0.shape[0]
    n_blk = S // ATT_BLK
    blk3 = (n_blk, ATT_BLK, HEAD_DIM)
    qi = lax.broadcasted_iota(jnp.int32, (1, ATT_BLK, ATT_BLK), 1)
    kj = lax.broadcasted_iota(jnp.int32, (1, ATT_BLK, ATT_BLK), 2)
    cur_ok = kj <= qi
    blk_id = lax.broadcasted_iota(jnp.int32, (n_blk, 1, 1), 0)
    qk_dims = (((2,), (2,)), ((0,), (0,)))
    pv_dims = (((2,), (1,)), ((0,), (0,)))

    for g, (q_ref, k_ref, v_ref) in enumerate(((q0, k0, v0), (q1, k1, v1), (q2, k2, v2))):
        dil = ATTN_PATTERNS[g][1]
        blocks_per_class = n_blk // dil
        q = q_ref[:, hd].reshape(blk3)
        k = k_ref[:, hd].reshape(blk3)
        v = v_ref[:, hd].reshape(blk3)
        v1 = jnp.concatenate([v, jnp.ones_like(v)], axis=2)
        if blocks_per_class > 1:
            shift = lambda t: jnp.concatenate([t[:1], t[:-1]], axis=0)
            kk = jnp.concatenate([shift(k), k], axis=1)
            vv = jnp.concatenate([shift(v1), v1], axis=1)
            s = lax.dot_general(q, kk, qk_dims, preferred_element_type=jnp.float32)
            first = (blk_id & (blocks_per_class - 1)) == 0
            prev_ok = kj >= qi + jnp.where(first, ATT_BLK, 0)
            s_prev = jnp.where(prev_ok, s[:, :, :ATT_BLK], NEG_INF)
            s_cur = jnp.where(cur_ok, s[:, :, ATT_BLK:], NEG_INF)
            m = jnp.max(jnp.maximum(s_prev, s_cur), axis=-1, keepdims=True)
            e = jnp.concatenate([jnp.exp2(s_prev - m), jnp.exp2(s_cur - m)], axis=2)
        else:
            vv = v1
            s = lax.dot_general(q, k, qk_dims, preferred_element_type=jnp.float32)
            s = jnp.where(cur_ok, s, NEG_INF)
            m = jnp.max(s, axis=-1, keepdims=True)
            e = jnp.exp2(s - m)
        od = lax.dot_general(e.astype(MM_DTYPE), vv, pv_dims,
                             preferred_element_type=jnp.float32)
        den = od[:, :, HEAD_DIM:]
        o = od[:, :, :HEAD_DIM] * (1.0 / den)
        lse = m + jnp.log2(den)
        if dil == 1:
            o_nat[g] = o.reshape(S, HEAD_DIM)
            l_nat[g] = lse.reshape(S, HEAD_DIM)
        else:
            for nb in range(n_blk):
                c, m0 = divmod(nb, blocks_per_class)
                idx = pl.ds(m0 * ATT_BLK * dil + c, ATT_BLK, stride=dil)
                o_nat[g, idx, :] = o[nb]
                l_nat[g, idx, :] = lse[nb]


def _attn_kernel(q0, k0, v0, q1, k1, v1, q2, k2, v2, sz_ref, y_ref, o_nat, l_nat):
    S = sz_ref.shape[0]
    for i in range(ATT_HEADS_PER_STEP):
        hd = slice(i * HEAD_DIM, (i + 1) * HEAD_DIM)
        _attn_head(hd, q0, k0, v0, q1, k1, v1, q2, k2, v2, o_nat.at[i], l_nat.at[i])

    def merge(i, carry):
        rows = pl.ds(pl.multiple_of(i * COMBINE_ROWS, COMBINE_ROWS), COMBINE_ROWS)
        for j in range(ATT_HEADS_PER_STEP):
            hd = slice(j * HEAD_DIM, (j + 1) * HEAD_DIM)
            ls = [l_nat[j, g, rows, :] for g in range(N_GROUPS)]
            m = functools.reduce(jnp.maximum, ls)
            ws = [jnp.exp2(l - m) for l in ls]
            den = functools.reduce(lambda p, q: p + q, ws)
            att = functools.reduce(
                lambda p, q: p + q, [w * o_nat[j, g, rows, :] for g, w in enumerate(ws)])
            att = att * (1.0 / den)
            y_ref[rows, hd] = (att * sz_ref[rows, hd].astype(jnp.float32)).astype(y_ref.dtype)
        return carry
    lax.fori_loop(0, S // COMBINE_ROWS, merge, 0)


def _attention(qkv0, qkv1, qkv2, sz):
    B, S, _ = qkv0.shape
    hps = ATT_HEADS_PER_STEP
    w = hps * HEAD_DIM
    qkv1 = qkv1.reshape(B, S, GROUP_W)
    qkv2 = qkv2.reshape(B, S, GROUP_W)
    in_specs = []
    for _ in range(N_GROUPS):
        for part in range(3):
            in_specs.append(pl.BlockSpec(
                (None, S, w), lambda b, h, part=part: (b, 0, part * (HEADS // hps) + h)))
    in_specs.append(pl.BlockSpec((None, S, w), lambda b, h: (b, 0, h)))
    return pl.pallas_call(
        _attn_kernel,
        grid=(B, HEADS // hps),
        in_specs=in_specs,
        out_specs=pl.BlockSpec((None, S, w), lambda b, h: (b, 0, h)),
        out_shape=jax.ShapeDtypeStruct((B, S, ATT_W), MM_DTYPE),
        scratch_shapes=[
            pltpu.VMEM((hps, N_GROUPS, S, HEAD_DIM), jnp.float32),
            pltpu.VMEM((hps, N_GROUPS, S, HEAD_DIM), jnp.float32),
        ],
        compiler_params=_params(2),
        name="attention",
    )(qkv0, qkv0, qkv0, qkv1, qkv1, qkv1, qkv2, qkv2, qkv2, sz)


def _out_kernel(x_ref, p_ref, ya_ref, yr_ref, g_ref, woa_ref, wout_ref, wpg_ref, bpg_ref,
                wple_ref, o_ref):
    tm = x_ref.shape[0]
    halves = [pl.ds(i * (tm // OUT_SUB), tm // OUT_SUB) for i in range(OUT_SUB)]
    staged = []
    for rows in halves:
        pe = _dot(p_ref[rows, :].astype(MM_DTYPE), wple_ref[...])
        ya = _dot(ya_ref[rows, :], woa_ref[...])
        g0 = g_ref[rows, :D_MODEL].astype(jnp.float32)
        g1 = g_ref[rows, D_MODEL:].astype(jnp.float32)
        merged = g0 * yr_ref[rows, :].astype(jnp.float32) + g1 * ya
        x2 = x_ref[rows, :] + _dot(merged.astype(MM_DTYPE), wout_ref[...])
        staged.append((x2, pe))
    for rows, (x2, pe) in zip(halves, staged):
        n2 = _rms_normalize(x2).astype(MM_DTYPE)
        pg = _sigmoid(_dot(n2, wpg_ref[...]) + bpg_ref[...])
        o_ref[rows, :] = x2 + pg * pe


def _out_proj(x, p, ya, yr, gates, woa, wout, wpg, bpg, wple):
    B, S, _ = x.shape
    tm = TM_OUT
    row = lambda w: pl.BlockSpec((None, tm, w), lambda b, i: (b, i, 0))
    return pl.pallas_call(
        _out_kernel,
        grid=(B, S // tm),
        in_specs=[
            row(D_MODEL), row(PLE_DIM), row(ATT_W), row(D_MODEL), row(2 * D_MODEL),
            _const_spec(woa.shape), _const_spec(wout.shape), _const_spec(wpg.shape),
            _const_spec(bpg.shape), _const_spec(wple.shape),
        ],
        out_specs=row(D_MODEL),
        out_shape=jax.ShapeDtypeStruct((B, S, D_MODEL), x.dtype),
        compiler_params=_params(2),
        name="out_proj",
    )(x, p, ya, yr, gates, woa, wout, wpg, bpg, wple)


def _rope_tables(s):
    ang = (np.arange(s, dtype=np.float64)[:, None]
           * ROPE_THETA ** (-np.arange(0, HEAD_DIM, 2, dtype=np.float64) / HEAD_DIM)[None, :])
    cos, sin = np.cos(ang), np.sin(ang)
    cos2 = np.concatenate([cos, cos], axis=1).astype(np.float32)
    sin2 = np.concatenate([-sin, sin], axis=1).astype(np.float32)
    tabs = []
    for _, dil in ATTN_PATTERNS:
        if dil == 1:
            tabs.append((jnp.asarray(cos2), jnp.asarray(sin2)))
        else:
            perm = lambda t: jnp.asarray(
                np.ascontiguousarray(t.reshape(s // dil, dil, HEAD_DIM).transpose(1, 0, 2)))
            tabs.append((perm(cos2), perm(sin2)))
    return tabs


def kernel(x, p, norm_mix, w_in, b_in, conv_w, conv_b, w_rg_a, b_rg_a, w_rg_x, b_rg_x,
           lru_lambda, q_norm, k_norm, w_o_rnn, w_o_att, w_out, norm_ple, w_ple_gate,
           b_ple_gate, w_ple):
    depth = w_in.shape[0]
    s = x.shape[1]
    tabs = _rope_tables(s)
    f32 = jnp.float32
    for layer in range(depth):
        w = w_in[layer] * norm_mix[layer].astype(f32)[:, None]
        b = b_in[layer].astype(f32)[None, :]
        cols = lambda a, lo, hi: a[:, lo:hi]
        grp = lambda a, g: cols(a, OFF_QKV + g * GROUP_W, OFF_QKV + (g + 1) * GROUP_W)
        wb = lambda lo, hi: (cols(w, lo, hi).astype(MM_DTYPE), cols(b, lo, hi))
        gq = q_norm[layer].astype(f32) * (HEAD_DIM ** -0.5 * LOG2_E)
        gk = k_norm[layer].astype(f32)
        attn_w = (grp(w, 0).astype(MM_DTYPE), grp(b, 0),
                  *wb(OFF_Z_ATT, OFF_GATES), *wb(OFF_GATES, w.shape[1]),
                  grp(w, 1).astype(MM_DTYPE), grp(b, 1),
                  grp(w, 2).astype(MM_DTYPE), grp(b, 2), gq, gk)
        wg = jnp.concatenate([w_rg_a[layer], w_rg_x[layer]], axis=2).astype(MM_DTYPE)
        bg = jnp.stack([b_rg_a[layer], b_rg_x[layer]], axis=0).astype(f32)
        rnn_w = (*wb(0, OFF_Z_RNN), *wb(OFF_Z_RNN, OFF_QKV), conv_w[layer].astype(f32),
                 conv_b[layer].astype(f32)[None, :], wg, bg,
                 lru_lambda[layer].astype(f32)[None, :], w_o_rnn[layer].astype(MM_DTYPE))
        qkv0, qkv1, qkv2, sz, gates = _in_proj_attn(x, attn_w, tabs)
        yr = _rnn_branch(x, rnn_w)
        ya = _attention(qkv0, qkv1, qkv2, sz)
        x = _out_proj(
            x, p[layer], ya, yr, gates,
            w_o_att[layer].astype(MM_DTYPE), w_out[layer].astype(MM_DTYPE),
            (w_ple_gate[layer] * norm_ple[layer].astype(f32)[:, None]).astype(MM_DTYPE),
            b_ple_gate[layer].astype(f32)[None, :], w_ple[layer].astype(MM_DTYPE))
    return x
```

```python
import functools
import math

import jax
import jax.numpy as jnp
import numpy as np
from jax import lax
from jax.experimental import pallas as pl
from jax.experimental.pallas import tpu as pltpu

D_MODEL = 1024
PLE_DIM = 256
D_RNN = 1280
RNN_BLOCKS = 10
RNN_BLOCK_W = D_RNN // RNN_BLOCKS
CONV_W = 4
LRU_C = 8.0
HEAD_DIM = 128
HEADS = 4
ATTN_PATTERNS = ((128, 1), (512, 4), (2048, 16))
N_GROUPS = len(ATTN_PATTERNS)
ATT_W = HEADS * HEAD_DIM
GROUP_W = 3 * ATT_W
ATT_BLK = 128
ROPE_THETA = 10000.0
EPS = 1e-6

OFF_Z_RNN = D_RNN
OFF_QKV = 2 * D_RNN
OFF_Z_ATT = OFF_QKV + N_GROUPS * GROUP_W
OFF_GATES = OFF_Z_ATT + ATT_W

LANES = 128
SUBLANES = 8
VMEM_LIMIT_BYTES = 56 * 1024 * 1024

TM_IN = 512
IN_SUB = 2
TT_RNN = 64
RNN_SUB_T = 16
TM_OUT = 1024
OUT_SUB = 4
COMBINE_ROWS = 256
ATT_HEADS_PER_STEP = 2

MM_DTYPE = jnp.bfloat16
LOG2_E = math.log2(math.e)
EXPM1_SERIES_BELOW = 2.0 ** -11
EXPM1_C1 = -2.0 * math.log(2.0)
EXPM1_C2 = -2.0 * math.log(2.0) ** 2
NEG_INF = float("-inf")


def _sigmoid(v):
    return 1.0 / (1.0 + jnp.exp2(v * (-LOG2_E)))


def _rms_normalize(v):
    var = jnp.mean(v * v, axis=-1, keepdims=True)
    return v * lax.rsqrt(var + EPS)


def _dot(a, b):
    return jnp.dot(a, b, preferred_element_type=jnp.float32)


def _const_spec(shape):
    zeros = (0,) * len(shape)
    return pl.BlockSpec(shape, lambda *_: zeros, pipeline_mode=pl.Buffered(1))


def _params(n_grid):
    return pltpu.CompilerParams(
        dimension_semantics=("arbitrary",) * n_grid,
        vmem_limit_bytes=VMEM_LIMIT_BYTES,
    )


def _qk_epilogue(acc, gq, gk, cos2, sin2):
    outs = []
    for part, gain in ((0, gq), (1, gk)):
        for h in range(HEADS):
            lo = part * ATT_W + h * HEAD_DIM
            t = _rms_normalize(acc[:, lo:lo + HEAD_DIM]) * gain
            outs.append(t * cos2 + pltpu.roll(t, HEAD_DIM // 2, 1) * sin2)
    outs.append(acc[:, 2 * ATT_W:])
    return jnp.concatenate(outs, axis=1)


def _attn_natural(hb, rows, wq0_ref, bq0_ref, wz_ref, bz_ref, gq_ref, gk_ref,
                  cos0_ref, sin0_ref, qkv0_ref, sz_ref):
    acc = _dot(hb, wq0_ref[...]) + bq0_ref[...]
    qkv0_ref[rows, :] = _qk_epilogue(acc, gq_ref[0:1], gk_ref[0:1], cos0_ref[rows, :],
                                     sin0_ref[rows, :]).astype(qkv0_ref.dtype)
    z = _dot(hb, wz_ref[...]) + bz_ref[...]
    sz_ref[rows, :] = (z * _sigmoid(z)).astype(sz_ref.dtype)


def _attn_dilated(hn, sub, w1_ref, b1_ref, w2_ref, b2_ref, gq_ref, gk_ref, cos1_ref, sin1_ref,
                  cos2_ref, sin2_ref, qkv1_ref, qkv2_ref, hs_ref):
    tm = hn.shape[0]
    n_slabs = D_MODEL // LANES
    for j in range(n_slabs):
        hs_ref[sub, j] = hn[:, j * LANES:(j + 1) * LANES]
    for (dil, w_ref, b_ref, gi, cos_ref, sin_ref, out_ref) in (
            (ATTN_PATTERNS[1][1], w1_ref, b1_ref, 1, cos1_ref, sin1_ref, qkv1_ref),
            (ATTN_PATTERNS[2][1], w2_ref, b2_ref, 2, cos2_ref, sin2_ref, qkv2_ref)):
        per = tm // dil
        classes = []
        for c in range(dil):
            classes.append(jnp.concatenate(
                [hs_ref[sub, j, pl.ds(c, per, stride=dil), :] for j in range(n_slabs)],
                axis=1))
        hp = jnp.concatenate(classes, axis=0).astype(MM_DTYPE)
        acc = _dot(hp, w_ref[...]) + b_ref[...]
        part = pl.ds(sub * per, per)
        cos2 = cos_ref[:, part, :].reshape(tm, HEAD_DIM)
        sin2 = sin_ref[:, part, :].reshape(tm, HEAD_DIM)
        res = _qk_epilogue(acc, gq_ref[gi:gi + 1], gk_ref[gi:gi + 1], cos2, sin2)
        out_ref[:, part, :] = res.astype(out_ref.dtype).reshape(dil, per, GROUP_W)


def _in_proj_attn_kernel(x_ref,
                         wq0_ref, bq0_ref, wz_ref, bz_ref, w1_ref, b1_ref,
                         w2_ref, b2_ref, gq_ref, gk_ref,
                         cos0_ref, sin0_ref, cos1_ref, sin1_ref, cos2_ref, sin2_ref,
                         qkv0_ref, qkv1_ref, qkv2_ref, sz_ref, hs_ref):
    ts = x_ref.shape[0] // IN_SUB
    for sub in range(IN_SUB):
        rows = pl.ds(sub * ts, ts)
        hn = _rms_normalize(x_ref[rows, :])
        _attn_natural(hn.astype(MM_DTYPE), rows, wq0_ref, bq0_ref, wz_ref, bz_ref, gq_ref,
                      gk_ref, cos0_ref, sin0_ref, qkv0_ref, sz_ref)
        _attn_dilated(hn, sub, w1_ref, b1_ref, w2_ref, b2_ref, gq_ref, gk_ref, cos1_ref,
                      sin1_ref, cos2_ref, sin2_ref, qkv1_ref, qkv2_ref, hs_ref)


def _in_proj_attn(x, attn_w, tabs):
    B, S, _ = x.shape
    tm = TM_IN
    d1, d2 = ATTN_PATTERNS[1][1], ATTN_PATTERNS[2][1]
    (cos0, sin0), (cos1, sin1), (cos2, sin2) = tabs
    row = lambda w: pl.BlockSpec((None, tm, w), lambda b, i: (b, i, 0))
    tab_specs = [
        pl.BlockSpec((tm, HEAD_DIM), lambda b, i: (i, 0)),
        pl.BlockSpec((tm, HEAD_DIM), lambda b, i: (i, 0)),
        pl.BlockSpec((d1, tm // d1, HEAD_DIM), lambda b, i: (0, i, 0)),
        pl.BlockSpec((d1, tm // d1, HEAD_DIM), lambda b, i: (0, i, 0)),
        pl.BlockSpec((d2, tm // d2, HEAD_DIM), lambda b, i: (0, i, 0)),
        pl.BlockSpec((d2, tm // d2, HEAD_DIM), lambda b, i: (0, i, 0)),
    ]
    in_specs = [row(D_MODEL)] + [_const_spec(a.shape) for a in attn_w] + tab_specs
    out_shape = (
        jax.ShapeDtypeStruct((B, S, GROUP_W), MM_DTYPE),
        jax.ShapeDtypeStruct((B, d1, S // d1, GROUP_W), MM_DTYPE),
        jax.ShapeDtypeStruct((B, d2, S // d2, GROUP_W), MM_DTYPE),
        jax.ShapeDtypeStruct((B, S, ATT_W), MM_DTYPE),
    )
    out_specs = (
        row(GROUP_W),
        pl.BlockSpec((None, d1, tm // d1, GROUP_W), lambda b, i: (b, 0, i, 0)),
        pl.BlockSpec((None, d2, tm // d2, GROUP_W), lambda b, i: (b, 0, i, 0)),
        row(ATT_W),
    )
    return pl.pallas_call(
        _in_proj_attn_kernel,
        grid=(B, S // tm),
        in_specs=in_specs,
        out_specs=out_specs,
        out_shape=out_shape,
        scratch_shapes=[pltpu.VMEM((IN_SUB, D_MODEL // LANES, tm // IN_SUB, LANES),
                                   jnp.float32)],
        compiler_params=_params(2),
        name="in_proj_attn",
    )(x, *attn_w, cos0, sin0, cos1, sin1, cos2, sin2)


def _rnn_kernel(x_ref, perm_ref, wx_ref, bx_ref, wz_ref, bz_ref, cw_ref, cb_ref, wg_ref, bg_ref,
                lam_ref, wo_ref, yr_ref, xs_ref, h_ref):
    nb, tt, _ = x_ref.shape
    ts = RNN_SUB_T
    n_sub = tt // ts
    rows = nb * ts
    hist = (CONV_W - 1) * nb
    step = pl.program_id(0)

    @pl.when(step == 0)
    def _():
        xs_ref[0:hist, :] = jnp.zeros((hist, D_RNN), jnp.float32)
        h_ref[...] = jnp.zeros_like(h_ref)

    neg_lam = -lam_ref[...]
    softplus = jnp.maximum(neg_lam, 0.0) + jnp.log1p(jnp.exp(-jnp.abs(neg_lam)))
    log2_a_scale = (-LRU_C * LOG2_E) * softplus

    def front(s):
        hn = jnp.concatenate(
            [_rms_normalize(x_ref[b, s * ts:(s + 1) * ts, :]) for b in range(nb)], axis=0)
        hp = _dot(perm_ref[0], hn.astype(MM_DTYPE)).astype(MM_DTYPE)
        xr = _dot(hp, wx_ref[...]) + bx_ref[...]
        z = _dot(hp, wz_ref[...]) + bz_ref[...]
        xs_ref[pl.ds(hist + s * rows, rows), :] = xr
        xc = cb_ref[...]
        for k in range(CONV_W):
            xc = xc + cw_ref[k:k + 1, :] * xs_ref[pl.ds(s * rows + k * nb, rows), :]
        xcb = xc.astype(MM_DTYPE)
        pre = [_dot(xcb[:, n * RNN_BLOCK_W:(n + 1) * RNN_BLOCK_W], wg_ref[n])
               for n in range(RNN_BLOCKS)]
        return xc, pre, z

    def back(s, staged, h):
        xc, pre, z = staged
        r = _sigmoid(jnp.concatenate([t[:, :RNN_BLOCK_W] for t in pre], axis=1)
                     + bg_ref[0:1, :])
        gi = _sigmoid(jnp.concatenate([t[:, RNN_BLOCK_W:] for t in pre], axis=1)
                      + bg_ref[1:2, :])
        t = log2_a_scale * r
        a = jnp.exp2(t)
        y1 = jnp.where(t > -EXPM1_SERIES_BELOW,
                       t * (EXPM1_C1 + EXPM1_C2 * t), 1.0 - a * a)
        mult = jnp.where(y1 > 0.0, y1 * lax.rsqrt(y1), 0.0)
        gx = gi * xc
        hs = []
        for t in range(ts):
            sl = slice(t * nb, (t + 1) * nb)
            m_t = mult[sl]
            if s == 0 and t == 0:
                m_t = jnp.where(step == 0, 1.0, m_t)
            h = a[sl] * h + m_t * gx[sl]
            hs.append(h)
        y = (jnp.concatenate(hs, axis=0) * (z * _sigmoid(z))).astype(MM_DTYPE)
        yr = _dot(y, wo_ref[...])
        yn = _dot(perm_ref[1], yr.astype(yr_ref.dtype)).astype(yr_ref.dtype)
        for b in range(nb):
            yr_ref[b, s * ts:(s + 1) * ts, :] = yn[b * ts:(b + 1) * ts]
        return h

    h = h_ref[...]
    staged = {0: front(0)}
    for s in range(n_sub):
        if s + 1 < n_sub:
            staged[s + 1] = front(s + 1)
        h = back(s, staged.pop(s), h)
    h_ref[...] = h
    xs_ref[0:hist, :] = xs_ref[pl.ds(n_sub * rows, hist), :]


def _time_major_perms(nb, ts):
    r = np.arange(nb * ts)
    p = np.zeros((nb * ts, nb * ts), np.float32)
    p[r, (r % nb) * ts + r // nb] = 1.0
    return jnp.asarray(np.stack([p, p.T]), MM_DTYPE)


def _rnn_branch(x, rnn_w):
    B, S, _ = x.shape
    tt = TT_RNN
    perms = _time_major_perms(B, RNN_SUB_T)
    return pl.pallas_call(
        _rnn_kernel,
        grid=(S // tt,),
        in_specs=([pl.BlockSpec((B, tt, D_MODEL), lambda i: (0, i, 0)),
                   _const_spec(perms.shape)]
                  + [_const_spec(a.shape) for a in rnn_w]),
        out_specs=pl.BlockSpec((B, tt, D_MODEL), lambda i: (0, i, 0)),
        out_shape=jax.ShapeDtypeStruct((B, S, D_MODEL), MM_DTYPE),
        scratch_shapes=[
            pltpu.VMEM(((CONV_W - 1) * B + B * tt, D_RNN), jnp.float32),
            pltpu.VMEM((B, D_RNN), jnp.float32),
        ],
        compiler_params=_params(1),
        name="rnn_branch",
    )(x, perms, *rnn_w)


def _attn_head(hd, q0, k0, v0, q1, k1, v1, q2, k2, v2, o_nat, l_nat):
    S = q0.shape[0]
    n_blk = S // ATT_BLK
    blk3 = (n_blk, ATT_BLK, HEAD_DIM)
    qi = lax.broadcasted_iota(jnp.int32, (1, ATT_BLK, ATT_BLK), 1)
    kj = lax.broadcasted_iota(jnp.int32, (1, ATT_BLK, ATT_BLK), 2)
    cur_ok = kj <= qi
    blk_id = lax.broadcasted_iota(jnp.int32, (n_blk, 1, 1), 0)
    qk_dims = (((2,), (2,)), ((0,), (0,)))
    pv_dims = (((2,), (1,)), ((0,), (0,)))

    for g, (q_ref, k_ref, v_ref) in enumerate(((q0, k0, v0), (q1, k1, v1), (q2, k2, v2))):
        dil = ATTN_PATTERNS[g][1]
        blocks_per_class = n_blk // dil
        q = q_ref[:, hd].reshape(blk3)
        k = k_ref[:, hd].reshape(blk3)
        v = v_ref[:, hd].reshape(blk3)
        v1 = jnp.concatenate([v, jnp.ones_like(v)], axis=2)
        if blocks_per_class > 1:
            shift = lambda t: jnp.concatenate([t[:1], t[:-1]], axis=0)
            kk = jnp.concatenate([shift(k), k], axis=1)
            vv = jnp.concatenate([shift(v1), v1], axis=1)
            s = lax.dot_general(q, kk, qk_dims, preferred_element_type=jnp.float32)
            first = (blk_id & (blocks_per_class - 1)) == 0
            prev_ok = kj >= qi + jnp.where(first, ATT_BLK, 0)
            s_prev = jnp.where(prev_ok, s[:, :, :ATT_BLK], NEG_INF)
            s_cur = jnp.where(cur_ok, s[:, :, ATT_BLK:], NEG_INF)
            m = jnp.max(jnp.maximum(s_prev, s_cur), axis=-1, keepdims=True)
            e = jnp.concatenate([jnp.exp2(s_prev - m), jnp.exp2(s_cur - m)], axis=2)
        else:
            vv = v1
            s = lax.dot_general(q, k, qk_dims, preferred_element_type=jnp.float32)
            s = jnp.where(cur_ok, s, NEG_INF)
            m = jnp.max(s, axis=-1, keepdims=True)
            e = jnp.exp2(s - m)
        od = lax.dot_general(e.astype(MM_DTYPE), vv, pv_dims,
                             preferred_element_type=jnp.float32)
        den = od[:, :, HEAD_DIM:]
        o = od[:, :, :HEAD_DIM] * (1.0 / den)
        lse = m + jnp.log2(den)
        if dil == 1:
            o_nat[g] = o.reshape(S, HEAD_DIM)
            l_nat[g] = lse.reshape(S, HEAD_DIM)
        else:
            for nb in range(n_blk):
                c, m0 = divmod(nb, blocks_per_class)
                idx = pl.ds(m0 * ATT_BLK * dil + c, ATT_BLK, stride=dil)
                o_nat[g, idx, :] = o[nb]
                l_nat[g, idx, :] = lse[nb]


def _attn_kernel(q0, k0, v0, q1, k1, v1, q2, k2, v2, sz_ref, y_ref, o_nat, l_nat):
    S = sz_ref.shape[0]
    for i in range(ATT_HEADS_PER_STEP):
        hd = slice(i * HEAD_DIM, (i + 1) * HEAD_DIM)
        _attn_head(hd, q0, k0, v0, q1, k1, v1, q2, k2, v2, o_nat.at[i], l_nat.at[i])

    def merge(i, carry):
        rows = pl.ds(pl.multiple_of(i * COMBINE_ROWS, COMBINE_ROWS), COMBINE_ROWS)
        for j in range(ATT_HEADS_PER_STEP):
            hd = slice(j * HEAD_DIM, (j + 1) * HEAD_DIM)
            ls = [l_nat[j, g, rows, :] for g in range(N_GROUPS)]
            m = functools.reduce(jnp.maximum, ls)
            ws = [jnp.exp2(l - m) for l in ls]
            den = functools.reduce(lambda p, q: p + q, ws)
            att = functools.reduce(
                lambda p, q: p + q, [w * o_nat[j, g, rows, :] for g, w in enumerate(ws)])
            att = att * (1.0 / den)
            y_ref[rows, hd] = (att * sz_ref[rows, hd].astype(jnp.float32)).astype(y_ref.dtype)
        return carry
    lax.fori_loop(0, S // COMBINE_ROWS, merge, 0)


def _attention(qkv0, qkv1, qkv2, sz):
    B, S, _ = qkv0.shape
    hps = ATT_HEADS_PER_STEP
    w = hps * HEAD_DIM
    qkv1 = qkv1.reshape(B, S, GROUP_W)
    qkv2 = qkv2.reshape(B, S, GROUP_W)
    in_specs = []
    for _ in range(N_GROUPS):
        for part in range(3):
            in_specs.append(pl.BlockSpec(
                (None, S, w), lambda b, h, part=part: (b, 0, part * (HEADS // hps) + h)))
    in_specs.append(pl.BlockSpec((None, S, w), lambda b, h: (b, 0, h)))
    return pl.pallas_call(
        _attn_kernel,
        grid=(B, HEADS // hps),
        in_specs=in_specs,
        out_specs=pl.BlockSpec((None, S, w), lambda b, h: (b, 0, h)),
        out_shape=jax.ShapeDtypeStruct((B, S, ATT_W), MM_DTYPE),
        scratch_shapes=[
            pltpu.VMEM((hps, N_GROUPS, S, HEAD_DIM), jnp.float32),
            pltpu.VMEM((hps, N_GROUPS, S, HEAD_DIM), jnp.float32),
        ],
        compiler_params=_params(2),
        name="attention",
    )(qkv0, qkv0, qkv0, qkv1, qkv1, qkv1, qkv2, qkv2, qkv2, sz)


def _out_kernel(x_ref, p_ref, ya_ref, yr_ref, wgt_ref, bgt_ref, woa_ref, wout_ref, wpg_ref,
                bpg_ref, wple_ref, o_ref):
    tm = x_ref.shape[0]
    halves = [pl.ds(i * (tm // OUT_SUB), tm // OUT_SUB) for i in range(OUT_SUB)]
    staged = []
    for rows in halves:
        pe = _dot(p_ref[rows, :].astype(MM_DTYPE), wple_ref[...])
        ya = _dot(ya_ref[rows, :], woa_ref[...])
        xin = x_ref[rows, :]
        g = _sigmoid(_dot(_rms_normalize(xin).astype(MM_DTYPE), wgt_ref[...]) + bgt_ref[...])
        merged = g[:, :D_MODEL] * yr_ref[rows, :].astype(jnp.float32) + g[:, D_MODEL:] * ya
        x2 = xin + _dot(merged.astype(MM_DTYPE), wout_ref[...])
        staged.append((x2, pe))
    for rows, (x2, pe) in zip(halves, staged):
        n2 = _rms_normalize(x2).astype(MM_DTYPE)
        pg = _sigmoid(_dot(n2, wpg_ref[...]) + bpg_ref[...])
        o_ref[rows, :] = x2 + pg * pe


def _out_proj(x, p, ya, yr, wgt, bgt, woa, wout, wpg, bpg, wple):
    B, S, _ = x.shape
    tm = TM_OUT
    row = lambda w: pl.BlockSpec((None, tm, w), lambda b, i: (b, i, 0))
    return pl.pallas_call(
        _out_kernel,
        grid=(B, S // tm),
        in_specs=[
            row(D_MODEL), row(PLE_DIM), row(ATT_W), row(D_MODEL),
            _const_spec(wgt.shape), _const_spec(bgt.shape), _const_spec(woa.shape), _const_spec(wout.shape), _const_spec(wpg.shape),
            _const_spec(bpg.shape), _const_spec(wple.shape),
        ],
        out_specs=row(D_MODEL),
        out_shape=jax.ShapeDtypeStruct((B, S, D_MODEL), x.dtype),
        compiler_params=_params(2),
        name="out_proj",
    )(x, p, ya, yr, wgt, bgt, woa, wout, wpg, bpg, wple)


def _rope_tables(s):
    ang = (np.arange(s, dtype=np.float64)[:, None]
           * ROPE_THETA ** (-np.arange(0, HEAD_DIM, 2, dtype=np.float64) / HEAD_DIM)[None, :])
    cos, sin = np.cos(ang), np.sin(ang)
    cos2 = np.concatenate([cos, cos], axis=1).astype(np.float32)
    sin2 = np.concatenate([-sin, sin], axis=1).astype(np.float32)
    tabs = []
    for _, dil in ATTN_PATTERNS:
        if dil == 1:
            tabs.append((jnp.asarray(cos2), jnp.asarray(sin2)))
        else:
            perm = lambda t: jnp.asarray(
                np.ascontiguousarray(t.reshape(s // dil, dil, HEAD_DIM).transpose(1, 0, 2)))
            tabs.append((perm(cos2), perm(sin2)))
    return tabs


def kernel(x, p, norm_mix, w_in, b_in, conv_w, conv_b, w_rg_a, b_rg_a, w_rg_x, b_rg_x,
           lru_lambda, q_norm, k_norm, w_o_rnn, w_o_att, w_out, norm_ple, w_ple_gate,
           b_ple_gate, w_ple):
    depth = w_in.shape[0]
    s = x.shape[1]
    tabs = _rope_tables(s)
    f32 = jnp.float32
    for layer in range(depth):
        w = w_in[layer] * norm_mix[layer].astype(f32)[:, None]
        b = b_in[layer].astype(f32)[None, :]
        cols = lambda a, lo, hi: a[:, lo:hi]
        grp = lambda a, g: cols(a, OFF_QKV + g * GROUP_W, OFF_QKV + (g + 1) * GROUP_W)
        wb = lambda lo, hi: (cols(w, lo, hi).astype(MM_DTYPE), cols(b, lo, hi))
        gq = q_norm[layer].astype(f32) * (HEAD_DIM ** -0.5 * LOG2_E)
        gk = k_norm[layer].astype(f32)
        attn_w = (grp(w, 0).astype(MM_DTYPE), grp(b, 0),
                  *wb(OFF_Z_ATT, OFF_GATES),
                  grp(w, 1).astype(MM_DTYPE), grp(b, 1),
                  grp(w, 2).astype(MM_DTYPE), grp(b, 2), gq, gk)
        wg = jnp.concatenate([w_rg_a[layer], w_rg_x[layer]], axis=2).astype(MM_DTYPE)
        bg = jnp.stack([b_rg_a[layer], b_rg_x[layer]], axis=0).astype(f32)
        rnn_w = (*wb(0, OFF_Z_RNN), *wb(OFF_Z_RNN, OFF_QKV), conv_w[layer].astype(f32),
                 conv_b[layer].astype(f32)[None, :], wg, bg,
                 lru_lambda[layer].astype(f32)[None, :], w_o_rnn[layer].astype(MM_DTYPE))
        qkv0, qkv1, qkv2, sz = _in_proj_attn(x, attn_w, tabs)
        yr = _rnn_branch(x, rnn_w)
        ya = _attention(qkv0, qkv1, qkv2, sz)
        x = _out_proj(
            x, p[layer], ya, yr, *wb(OFF_GATES, w.shape[1]),
            w_o_att[layer].astype(MM_DTYPE), w_out[layer].astype(MM_DTYPE),
            (w_ple_gate[layer] * norm_ple[layer].astype(f32)[:, None]).astype(MM_DTYPE),
            b_ple_gate[layer].astype(f32)[None, :], w_ple[layer].astype(MM_DTYPE))
    return x
```

```python
import functools
import math

import jax
import jax.numpy as jnp
import numpy as np
from jax import lax
from jax.experimental import pallas as pl
from jax.experimental.pallas import tpu as pltpu

D_MODEL = 1024
PLE_DIM = 256
D_RNN = 1280
RNN_BLOCKS = 10
RNN_BLOCK_W = D_RNN // RNN_BLOCKS
CONV_W = 4
LRU_C = 8.0
HEAD_DIM = 128
HEADS = 4
ATTN_PATTERNS = ((128, 1), (512, 4), (2048, 16))
N_GROUPS = len(ATTN_PATTERNS)
ATT_W = HEADS * HEAD_DIM
GROUP_W = 3 * ATT_W
ATT_BLK = 128
ROPE_THETA = 10000.0
EPS = 1e-6

OFF_Z_RNN = D_RNN
OFF_QKV = 2 * D_RNN
OFF_Z_ATT = OFF_QKV + N_GROUPS * GROUP_W
OFF_GATES = OFF_Z_ATT + ATT_W

LANES = 128
SUBLANES = 8
VMEM_LIMIT_BYTES = 56 * 1024 * 1024

TM_IN = 512
IN_SUB = 2
TT_RNN = 64
RNN_SUB_T = 16
TM_OUT = 1024
OUT_SUB = 4
COMBINE_ROWS = 256
ATT_HEADS_PER_STEP = 2
UNPERM_STEP = 4

MM_DTYPE = jnp.bfloat16
LOG2_E = math.log2(math.e)
EXPM1_SERIES_BELOW = 2.0 ** -11
EXPM1_C1 = -2.0 * math.log(2.0)
EXPM1_C2 = -2.0 * math.log(2.0) ** 2
NEG_INF = float("-inf")


def _sigmoid(v):
    return 1.0 / (1.0 + jnp.exp2(v * (-LOG2_E)))


def _rms_normalize(v):
    var = jnp.mean(v * v, axis=-1, keepdims=True)
    return v * lax.rsqrt(var + EPS)


def _dot(a, b):
    return jnp.dot(a, b, preferred_element_type=jnp.float32)


def _const_spec(shape):
    zeros = (0,) * len(shape)
    return pl.BlockSpec(shape, lambda *_: zeros, pipeline_mode=pl.Buffered(1))


def _params(n_grid):
    return pltpu.CompilerParams(
        dimension_semantics=("arbitrary",) * n_grid,
        vmem_limit_bytes=VMEM_LIMIT_BYTES,
    )


def _qk_epilogue(acc, gq, gk, cos2, sin2):
    outs = []
    for part, gain in ((0, gq), (1, gk)):
        for h in range(HEADS):
            lo = part * ATT_W + h * HEAD_DIM
            t = _rms_normalize(acc[:, lo:lo + HEAD_DIM]) * gain
            outs.append(t * cos2 + pltpu.roll(t, HEAD_DIM // 2, 1) * sin2)
    outs.append(acc[:, 2 * ATT_W:])
    return jnp.concatenate(outs, axis=1)


def _attn_natural(hb, rows, wq0_ref, bq0_ref, wz_ref, bz_ref, wgt_ref, bgt_ref, gq_ref, gk_ref,
                  cos0_ref, sin0_ref, qkv0_ref, sz_ref, gates_ref):
    acc = _dot(hb, wq0_ref[...]) + bq0_ref[...]
    qkv0_ref[rows, :] = _qk_epilogue(acc, gq_ref[0:1], gk_ref[0:1], cos0_ref[rows, :],
                                     sin0_ref[rows, :]).astype(qkv0_ref.dtype)
    z = _dot(hb, wz_ref[...]) + bz_ref[...]
    sz_ref[rows, :] = (z * _sigmoid(z)).astype(sz_ref.dtype)
    g = _dot(hb, wgt_ref[...]) + bgt_ref[...]
    gates_ref[rows, :] = _sigmoid(g).astype(gates_ref.dtype)


def _attn_dilated(hn, sub, w1_ref, b1_ref, w2_ref, b2_ref, gq_ref, gk_ref, cos1_ref, sin1_ref,
                  cos2_ref, sin2_ref, qkv1_ref, qkv2_ref, hs_ref):
    tm = hn.shape[0]
    n_slabs = D_MODEL // LANES
    for j in range(n_slabs):
        hs_ref[sub, j] = hn[:, j * LANES:(j + 1) * LANES]
    for (dil, w_ref, b_ref, gi, cos_ref, sin_ref, out_ref) in (
            (ATTN_PATTERNS[1][1], w1_ref, b1_ref, 1, cos1_ref, sin1_ref, qkv1_ref),
            (ATTN_PATTERNS[2][1], w2_ref, b2_ref, 2, cos2_ref, sin2_ref, qkv2_ref)):
        per = tm // dil
        classes = []
        for c in range(dil):
            classes.append(jnp.concatenate(
                [hs_ref[sub, j, pl.ds(c, per, stride=dil), :] for j in range(n_slabs)],
                axis=1))
        hp = jnp.concatenate(classes, axis=0).astype(MM_DTYPE)
        acc = _dot(hp, w_ref[...]) + b_ref[...]
        part = pl.ds(sub * per, per)
        cos2 = cos_ref[:, part, :].reshape(tm, HEAD_DIM)
        sin2 = sin_ref[:, part, :].reshape(tm, HEAD_DIM)
        res = _qk_epilogue(acc, gq_ref[gi:gi + 1], gk_ref[gi:gi + 1], cos2, sin2)
        out_ref[:, part, :] = res.astype(out_ref.dtype).reshape(dil, per, GROUP_W)


def _in_proj_attn_kernel(x_ref,
                         wq0_ref, bq0_ref, wz_ref, bz_ref, wgt_ref, bgt_ref, w1_ref, b1_ref,
                         w2_ref, b2_ref, gq_ref, gk_ref,
                         cos0_ref, sin0_ref, cos1_ref, sin1_ref, cos2_ref, sin2_ref,
                         qkv0_ref, qkv1_ref, qkv2_ref, sz_ref, gates_ref, hs_ref):
    ts = x_ref.shape[0] // IN_SUB
    for sub in range(IN_SUB):
        rows = pl.ds(sub * ts, ts)
        hn = _rms_normalize(x_ref[rows, :])
        _attn_natural(hn.astype(MM_DTYPE), rows, wq0_ref, bq0_ref, wz_ref, bz_ref, wgt_ref,
                      bgt_ref, gq_ref, gk_ref, cos0_ref, sin0_ref, qkv0_ref, sz_ref, gates_ref)
        _attn_dilated(hn, sub, w1_ref, b1_ref, w2_ref, b2_ref, gq_ref, gk_ref, cos1_ref,
                      sin1_ref, cos2_ref, sin2_ref, qkv1_ref, qkv2_ref, hs_ref)


def _in_proj_attn(x, attn_w, tabs):
    B, S, _ = x.shape
    tm = TM_IN
    d1, d2 = ATTN_PATTERNS[1][1], ATTN_PATTERNS[2][1]
    (cos0, sin0), (cos1, sin1), (cos2, sin2) = tabs
    row = lambda w: pl.BlockSpec((None, tm, w), lambda b, i: (b, i, 0))
    tab_specs = [
        pl.BlockSpec((tm, HEAD_DIM), lambda b, i: (i, 0)),
        pl.BlockSpec((tm, HEAD_DIM), lambda b, i: (i, 0)),
        pl.BlockSpec((d1, tm // d1, HEAD_DIM), lambda b, i: (0, i, 0)),
        pl.BlockSpec((d1, tm // d1, HEAD_DIM), lambda b, i: (0, i, 0)),
        pl.BlockSpec((d2, tm // d2, HEAD_DIM), lambda b, i: (0, i, 0)),
        pl.BlockSpec((d2, tm // d2, HEAD_DIM), lambda b, i: (0, i, 0)),
    ]
    in_specs = [row(D_MODEL)] + [_const_spec(a.shape) for a in attn_w] + tab_specs
    out_shape = (
        jax.ShapeDtypeStruct((B, S, GROUP_W), MM_DTYPE),
        jax.ShapeDtypeStruct((B, d1, S // d1, GROUP_W), MM_DTYPE),
        jax.ShapeDtypeStruct((B, d2, S // d2, GROUP_W), MM_DTYPE),
        jax.ShapeDtypeStruct((B, S, ATT_W), MM_DTYPE),
        jax.ShapeDtypeStruct((B, S, 2 * D_MODEL), MM_DTYPE),
    )
    out_specs = (
        row(GROUP_W),
        pl.BlockSpec((None, d1, tm // d1, GROUP_W), lambda b, i: (b, 0, i, 0)),
        pl.BlockSpec((None, d2, tm // d2, GROUP_W), lambda b, i: (b, 0, i, 0)),
        row(ATT_W),
        row(2 * D_MODEL),
    )
    return pl.pallas_call(
        _in_proj_attn_kernel,
        grid=(B, S // tm),
        in_specs=in_specs,
        out_specs=out_specs,
        out_shape=out_shape,
        scratch_shapes=[pltpu.VMEM((IN_SUB, D_MODEL // LANES, tm // IN_SUB, LANES),
                                   jnp.float32)],
        compiler_params=_params(2),
        name="in_proj_attn",
    )(x, *attn_w, cos0, sin0, cos1, sin1, cos2, sin2)


def _rnn_kernel(x_ref, perm_ref, wx_ref, bx_ref, wz_ref, bz_ref, cw_ref, cb_ref, wg_ref, bg_ref,
                lam_ref, wo_ref, yr_ref, xs_ref, h_ref):
    nb, tt, _ = x_ref.shape
    ts = RNN_SUB_T
    n_sub = tt // ts
    rows = nb * ts
    hist = (CONV_W - 1) * nb
    step = pl.program_id(0)

    @pl.when(step == 0)
    def _():
        xs_ref[0:hist, :] = jnp.zeros((hist, D_RNN), jnp.float32)
        h_ref[...] = jnp.zeros_like(h_ref)

    neg_lam = -lam_ref[...]
    softplus = jnp.maximum(neg_lam, 0.0) + jnp.log1p(jnp.exp(-jnp.abs(neg_lam)))
    log2_a_scale = (-LRU_C * LOG2_E) * softplus

    def front(s):
        hn = jnp.concatenate(
            [_rms_normalize(x_ref[b, s * ts:(s + 1) * ts, :]) for b in range(nb)], axis=0)
        hp = _dot(perm_ref[0], hn.astype(MM_DTYPE)).astype(MM_DTYPE)
        xr = _dot(hp, wx_ref[...]) + bx_ref[...]
        z = _dot(hp, wz_ref[...]) + bz_ref[...]
        xs_ref[pl.ds(hist + s * rows, rows), :] = xr
        xc = cb_ref[...]
        for k in range(CONV_W):
            xc = xc + cw_ref[k:k + 1, :] * xs_ref[pl.ds(s * rows + k * nb, rows), :]
        xcb = xc.astype(MM_DTYPE)
        pre = [_dot(xcb[:, n * RNN_BLOCK_W:(n + 1) * RNN_BLOCK_W], wg_ref[n])
               for n in range(RNN_BLOCKS)]
        return xc, pre, z

    def back(s, staged, h):
        xc, pre, z = staged
        r = _sigmoid(jnp.concatenate([t[:, :RNN_BLOCK_W] for t in pre], axis=1)
                     + bg_ref[0:1, :])
        gi = _sigmoid(jnp.concatenate([t[:, RNN_BLOCK_W:] for t in pre], axis=1)
                      + bg_ref[1:2, :])
        t = log2_a_scale * r
        a = jnp.exp2(t)
        y1 = jnp.where(t > -EXPM1_SERIES_BELOW,
                       t * (EXPM1_C1 + EXPM1_C2 * t), 1.0 - a * a)
        mult = jnp.where(y1 > 0.0, y1 * lax.rsqrt(y1), 0.0)
        gx = gi * xc
        hs = []
        for t in range(ts):
            sl = slice(t * nb, (t + 1) * nb)
            m_t = mult[sl]
            if s == 0 and t == 0:
                m_t = jnp.where(step == 0, 1.0, m_t)
            h = a[sl] * h + m_t * gx[sl]
            hs.append(h)
        y = (jnp.concatenate(hs, axis=0) * (z * _sigmoid(z))).astype(MM_DTYPE)
        yr = _dot(y, wo_ref[...])
        yn = _dot(perm_ref[1], yr.astype(yr_ref.dtype)).astype(yr_ref.dtype)
        for b in range(nb):
            yr_ref[b, s * ts:(s + 1) * ts, :] = yn[b * ts:(b + 1) * ts]
        return h

    h = h_ref[...]
    staged = {0: front(0)}
    for s in range(n_sub):
        if s + 1 < n_sub:
            staged[s + 1] = front(s + 1)
        h = back(s, staged.pop(s), h)
    h_ref[...] = h
    xs_ref[0:hist, :] = xs_ref[pl.ds(n_sub * rows, hist), :]


def _time_major_perms(nb, ts):
    r = np.arange(nb * ts)
    p = np.zeros((nb * ts, nb * ts), np.float32)
    p[r, (r % nb) * ts + r // nb] = 1.0
    return jnp.asarray(np.stack([p, p.T]), MM_DTYPE)


def _rnn_branch(x, rnn_w):
    B, S, _ = x.shape
    tt = TT_RNN
    perms = _time_major_perms(B, RNN_SUB_T)
    return pl.pallas_call(
        _rnn_kernel,
        grid=(S // tt,),
        in_specs=([pl.BlockSpec((B, tt, D_MODEL), lambda i: (0, i, 0)),
                   _const_spec(perms.shape)]
                  + [_const_spec(a.shape) for a in rnn_w]),
        out_specs=pl.BlockSpec((B, tt, D_MODEL), lambda i: (0, i, 0)),
        out_shape=jax.ShapeDtypeStruct((B, S, D_MODEL), MM_DTYPE),
        scratch_shapes=[
            pltpu.VMEM(((CONV_W - 1) * B + B * tt, D_RNN), jnp.float32),
            pltpu.VMEM((B, D_RNN), jnp.float32),
        ],
        compiler_params=_params(1),
        name="rnn_branch",
    )(x, perms, *rnn_w)


def _attn_head(hd, q0, k0, v0, q1, k1, v1, q2, k2, v2, o_nat, l_nat, tmp_o, tmp_l):
    S = q0.shape[0]
    n_blk = S // ATT_BLK
    blk3 = (n_blk, ATT_BLK, HEAD_DIM)
    qi = lax.broadcasted_iota(jnp.int32, (1, ATT_BLK, ATT_BLK), 1)
    kj = lax.broadcasted_iota(jnp.int32, (1, ATT_BLK, ATT_BLK), 2)
    cur_ok = kj <= qi
    blk_id = lax.broadcasted_iota(jnp.int32, (n_blk, 1, 1), 0)
    qk_dims = (((2,), (2,)), ((0,), (0,)))
    pv_dims = (((2,), (1,)), ((0,), (0,)))

    for g, (q_ref, k_ref, v_ref) in enumerate(((q0, k0, v0), (q1, k1, v1), (q2, k2, v2))):
        dil = ATTN_PATTERNS[g][1]
        blocks_per_class = n_blk // dil
        q = q_ref[:, hd].reshape(blk3)
        k = k_ref[:, hd].reshape(blk3)
        v = v_ref[:, hd].reshape(blk3)
        v1 = jnp.concatenate([v, jnp.ones_like(v)], axis=2)
        if blocks_per_class > 1:
            shift = lambda t: jnp.concatenate([t[:1], t[:-1]], axis=0)
            kk = jnp.concatenate([shift(k), k], axis=1)
            vv = jnp.concatenate([shift(v1), v1], axis=1)
            s = lax.dot_general(q, kk, qk_dims, preferred_element_type=jnp.float32)
            first = (blk_id & (blocks_per_class - 1)) == 0
            prev_ok = kj >= qi + jnp.where(first, ATT_BLK, 0)
            s_prev = jnp.where(prev_ok, s[:, :, :ATT_BLK], NEG_INF)
            s_cur = jnp.where(cur_ok, s[:, :, ATT_BLK:], NEG_INF)
            m = jnp.max(jnp.maximum(s_prev, s_cur), axis=-1, keepdims=True)
            e = jnp.concatenate([jnp.exp2(s_prev - m), jnp.exp2(s_cur - m)], axis=2)
        else:
            vv = v1
            s = lax.dot_general(q, k, qk_dims, preferred_element_type=jnp.float32)
            s = jnp.where(cur_ok, s, NEG_INF)
            m = jnp.max(s, axis=-1, keepdims=True)
            e = jnp.exp2(s - m)
        od = lax.dot_general(e.astype(MM_DTYPE), vv, pv_dims,
                             preferred_element_type=jnp.float32)
        den = od[:, :, HEAD_DIM:]
        o = od[:, :, :HEAD_DIM] * (1.0 / den)
        lse = m + jnp.log2(den)
        if dil == 1:
            o_nat[g] = o.reshape(S, HEAD_DIM)
            l_nat[g] = lse.reshape(S, HEAD_DIM)
        elif blocks_per_class > 1 or dil % UNPERM_STEP:
            for nb in range(n_blk):
                c, m0 = divmod(nb, blocks_per_class)
                idx = pl.ds(m0 * ATT_BLK * dil + c, ATT_BLK, stride=dil)
                o_nat[g, idx, :] = o[nb]
                l_nat[g, idx, :] = lse[nb]
        else:
            for nb in range(n_blk):
                c_hi, c_lo = divmod(nb, UNPERM_STEP)
                idx = pl.ds(c_hi, ATT_BLK, stride=dil // UNPERM_STEP)
                tmp_o[c_lo, idx, :] = o[nb]
                tmp_l[c_lo, idx, :] = lse[nb]
            for c_lo in range(UNPERM_STEP):
                idx = pl.ds(c_lo, S // UNPERM_STEP, stride=UNPERM_STEP)
                o_nat[g, idx, :] = tmp_o[c_lo]
                l_nat[g, idx, :] = tmp_l[c_lo]


def _attn_kernel(q0, k0, v0, q1, k1, v1, q2, k2, v2, sz_ref, y_ref, o_nat, l_nat, tmp_o,
                 tmp_l):
    S = sz_ref.shape[0]
    for i in range(ATT_HEADS_PER_STEP):
        hd = slice(i * HEAD_DIM, (i + 1) * HEAD_DIM)
        _attn_head(hd, q0, k0, v0, q1, k1, v1, q2, k2, v2, o_nat.at[i], l_nat.at[i],
                   tmp_o.at[i], tmp_l.at[i])

    def merge(i, carry):
        rows = pl.ds(pl.multiple_of(i * COMBINE_ROWS, COMBINE_ROWS), COMBINE_ROWS)
        for j in range(ATT_HEADS_PER_STEP):
            hd = slice(j * HEAD_DIM, (j + 1) * HEAD_DIM)
            ls = [l_nat[j, g, rows, :] for g in range(N_GROUPS)]
            m = functools.reduce(jnp.maximum, ls)
            ws = [jnp.exp2(l - m) for l in ls]
            den = functools.reduce(lambda p, q: p + q, ws)
            att = functools.reduce(
                lambda p, q: p + q, [w * o_nat[j, g, rows, :] for g, w in enumerate(ws)])
            att = att * (1.0 / den)
            y_ref[rows, hd] = (att * sz_ref[rows, hd].astype(jnp.float32)).astype(y_ref.dtype)
        return carry
    lax.fori_loop(0, S // COMBINE_ROWS, merge, 0)


def _attention(qkv0, qkv1, qkv2, sz):
    B, S, _ = qkv0.shape
    hps = ATT_HEADS_PER_STEP
    w = hps * HEAD_DIM
    qkv1 = qkv1.reshape(B, S, GROUP_W)
    qkv2 = qkv2.reshape(B, S, GROUP_W)
    in_specs = []
    for _ in range(N_GROUPS):
        for part in range(3):
            in_specs.append(pl.BlockSpec(
                (None, S, w), lambda b, h, part=part: (b, 0, part * (HEADS // hps) + h)))
    in_specs.append(pl.BlockSpec((None, S, w), lambda b, h: (b, 0, h)))
    return pl.pallas_call(
        _attn_kernel,
        grid=(B, HEADS // hps),
        in_specs=in_specs,
        out_specs=pl.BlockSpec((None, S, w), lambda b, h: (b, 0, h)),
        out_shape=jax.ShapeDtypeStruct((B, S, ATT_W), MM_DTYPE),
        scratch_shapes=[
            pltpu.VMEM((hps, N_GROUPS, S, HEAD_DIM), jnp.float32),
            pltpu.VMEM((hps, N_GROUPS, S, HEAD_DIM), jnp.float32),
            pltpu.VMEM((hps, UNPERM_STEP, S // UNPERM_STEP, HEAD_DIM), jnp.float32),
            pltpu.VMEM((hps, UNPERM_STEP, S // UNPERM_STEP, HEAD_DIM), jnp.float32),
        ],
        compiler_params=_params(2),
        name="attention",
    )(qkv0, qkv0, qkv0, qkv1, qkv1, qkv1, qkv2, qkv2, qkv2, sz)


def _out_kernel(x_ref, p_ref, ya_ref, yr_ref, g_ref, woa_ref, wout_ref, wpg_ref, bpg_ref,
                wple_ref, o_ref):
    tm = x_ref.shape[0]
    halves = [pl.ds(i * (tm // OUT_SUB), tm // OUT_SUB) for i in range(OUT_SUB)]
    staged = []
    for rows in halves:
        pe = _dot(p_ref[rows, :].astype(MM_DTYPE), wple_ref[...])
        ya = _dot(ya_ref[rows, :], woa_ref[...])
        g0 = g_ref[rows, :D_MODEL].astype(jnp.float32)
        g1 = g_ref[rows, D_MODEL:].astype(jnp.float32)
        merged = g0 * yr_ref[rows, :].astype(jnp.float32) + g1 * ya
        x2 = x_ref[rows, :] + _dot(merged.astype(MM_DTYPE), wout_ref[...])
        staged.append((x2, pe))
    for rows, (x2, pe) in zip(halves, staged):
        n2 = _rms_normalize(x2).astype(MM_DTYPE)
        pg = _sigmoid(_dot(n2, wpg_ref[...]) + bpg_ref[...])
        o_ref[rows, :] = x2 + pg * pe


def _out_proj(x, p, ya, yr, gates, woa, wout, wpg, bpg, wple):
    B, S, _ = x.shape
    tm = TM_OUT
    row = lambda w: pl.BlockSpec((None, tm, w), lambda b, i: (b, i, 0))
    return pl.pallas_call(
        _out_kernel,
        grid=(B, S // tm),
        in_specs=[
            row(D_MODEL), row(PLE_DIM), row(ATT_W), row(D_MODEL), row(2 * D_MODEL),
            _const_spec(woa.shape), _const_spec(wout.shape), _const_spec(wpg.shape),
            _const_spec(bpg.shape), _const_spec(wple.shape),
        ],
        out_specs=row(D_MODEL),
        out_shape=jax.ShapeDtypeStruct((B, S, D_MODEL), x.dtype),
        compiler_params=_params(2),
        name="out_proj",
    )(x, p, ya, yr, gates, woa, wout, wpg, bpg, wple)


def _rope_tables(s):
    ang = (np.arange(s, dtype=np.float64)[:, None]
           * ROPE_THETA ** (-np.arange(0, HEAD_DIM, 2, dtype=np.float64) / HEAD_DIM)[None, :])
    cos, sin = np.cos(ang), np.sin(ang)
    cos2 = np.concatenate([cos, cos], axis=1).astype(np.float32)
    sin2 = np.concatenate([-sin, sin], axis=1).astype(np.float32)
    tabs = []
    for _, dil in ATTN_PATTERNS:
        if dil == 1:
            tabs.append((jnp.asarray(cos2), jnp.asarray(sin2)))
        else:
            perm = lambda t: jnp.asarray(
                np.ascontiguousarray(t.reshape(s // dil, dil, HEAD_DIM).transpose(1, 0, 2)))
            tabs.append((perm(cos2), perm(sin2)))
    return tabs


def kernel(x, p, norm_mix, w_in, b_in, conv_w, conv_b, w_rg_a, b_rg_a, w_rg_x, b_rg_x,
           lru_lambda, q_norm, k_norm, w_o_rnn, w_o_att, w_out, norm_ple, w_ple_gate,
           b_ple_gate, w_ple):
    depth = w_in.shape[0]
    s = x.shape[1]
    tabs = _rope_tables(s)
    f32 = jnp.float32
    for layer in range(depth):
        w = w_in[layer] * norm_mix[layer].astype(f32)[:, None]
        b = b_in[layer].astype(f32)[None, :]
        cols = lambda a, lo, hi: a[:, lo:hi]
        grp = lambda a, g: cols(a, OFF_QKV + g * GROUP_W, OFF_QKV + (g + 1) * GROUP_W)
        wb = lambda lo, hi: (cols(w, lo, hi).astype(MM_DTYPE), cols(b, lo, hi))
        gq = q_norm[layer].astype(f32) * (HEAD_DIM ** -0.5 * LOG2_E)
        gk = k_norm[layer].astype(f32)
        attn_w = (grp(w, 0).astype(MM_DTYPE), grp(b, 0),
                  *wb(OFF_Z_ATT, OFF_GATES), *wb(OFF_GATES, w.shape[1]),
                  grp(w, 1).astype(MM_DTYPE), grp(b, 1),
                  grp(w, 2).astype(MM_DTYPE), grp(b, 2), gq, gk)
        wg = jnp.concatenate([w_rg_a[layer], w_rg_x[layer]], axis=2).astype(MM_DTYPE)
        bg = jnp.stack([b_rg_a[layer], b_rg_x[layer]], axis=0).astype(f32)
        rnn_w = (*wb(0, OFF_Z_RNN), *wb(OFF_Z_RNN, OFF_QKV), conv_w[layer].astype(f32),
                 conv_b[layer].astype(f32)[None, :], wg, bg,
                 lru_lambda[layer].astype(f32)[None, :], w_o_rnn[layer].astype(MM_DTYPE))
        qkv0, qkv1, qkv2, sz, gates = _in_proj_attn(x, attn_w, tabs)
        yr = _rnn_branch(x, rnn_w)
        ya = _attention(qkv0, qkv1, qkv2, sz)
        x = _out_proj(
            x, p[layer], ya, yr, gates,
            w_o_att[layer].astype(MM_DTYPE), w_out[layer].astype(MM_DTYPE),
            (w_ple_gate[layer] * norm_ple[layer].astype(f32)[:, None]).astype(MM_DTYPE),
            b_ple_gate[layer].astype(f32)[None, :], w_ple[layer].astype(MM_DTYPE))
    return x
```

```python
import functools
import math

import jax
import jax.numpy as jnp
import numpy as np
from jax import lax
from jax.experimental import pallas as pl
from jax.experimental.pallas import tpu as pltpu

D_MODEL = 1024
PLE_DIM = 256
D_RNN = 1280
RNN_BLOCKS = 10
RNN_BLOCK_W = D_RNN // RNN_BLOCKS
CONV_W = 4
LRU_C = 8.0
HEAD_DIM = 128
HEADS = 4
ATTN_PATTERNS = ((128, 1), (512, 4), (2048, 16))
N_GROUPS = len(ATTN_PATTERNS)
ATT_W = HEADS * HEAD_DIM
GROUP_W = 3 * ATT_W
ATT_BLK = 128
ROPE_THETA = 10000.0
EPS = 1e-6

OFF_Z_RNN = D_RNN
OFF_QKV = 2 * D_RNN
OFF_Z_ATT = OFF_QKV + N_GROUPS * GROUP_W
OFF_GATES = OFF_Z_ATT + ATT_W

LANES = 128
SUBLANES = 8
VMEM_LIMIT_BYTES = 56 * 1024 * 1024

TM_IN = 512
IN_SUB = 2
TT_RNN = 64
RNN_SUB_T = 16
TM_OUT = 1024
OUT_SUB = 4
COMBINE_ROWS = 256
ATT_HEADS_PER_STEP = 2
UNPERM_STEP = 4

MM_DTYPE = jnp.bfloat16
LOG2_E = math.log2(math.e)
EXPM1_SERIES_BELOW = 2.0 ** -11
EXPM1_C1 = -2.0 * math.log(2.0)
EXPM1_C2 = -2.0 * math.log(2.0) ** 2
NEG_INF = float("-inf")


def _sigmoid(v):
    return 1.0 / (1.0 + jnp.exp2(v * (-LOG2_E)))


def _rms_normalize(v):
    var = jnp.mean(v * v, axis=-1, keepdims=True)
    return v * lax.rsqrt(var + EPS)


def _dot(a, b):
    return jnp.dot(a, b, preferred_element_type=jnp.float32)


def _const_spec(shape):
    zeros = (0,) * len(shape)
    return pl.BlockSpec(shape, lambda *_: zeros, pipeline_mode=pl.Buffered(1))


def _params(n_grid):
    return pltpu.CompilerParams(
        dimension_semantics=("arbitrary",) * n_grid,
        vmem_limit_bytes=VMEM_LIMIT_BYTES,
    )


def _qk_epilogue(acc, gq, gk, cos2, sin2):
    outs = []
    for part, gain in ((0, gq), (1, gk)):
        for h in range(HEADS):
            lo = part * ATT_W + h * HEAD_DIM
            t = _rms_normalize(acc[:, lo:lo + HEAD_DIM]) * gain
            outs.append(t * cos2 + pltpu.roll(t, HEAD_DIM // 2, 1) * sin2)
    outs.append(acc[:, 2 * ATT_W:])
    return jnp.concatenate(outs, axis=1)


def _attn_natural(hb, rows, wq0_ref, bq0_ref, wz_ref, bz_ref, wgt_ref, bgt_ref, gq_ref, gk_ref,
                  cos0_ref, sin0_ref, qkv0_ref, sz_ref, gates_ref):
    acc = _dot(hb, wq0_ref[...]) + bq0_ref[...]
    qkv0_ref[rows, :] = _qk_epilogue(acc, gq_ref[0:1], gk_ref[0:1], cos0_ref[rows, :],
                                     sin0_ref[rows, :]).astype(qkv0_ref.dtype)
    z = _dot(hb, wz_ref[...]) + bz_ref[...]
    sz_ref[rows, :] = (z * _sigmoid(z)).astype(sz_ref.dtype)
    g = _dot(hb, wgt_ref[...]) + bgt_ref[...]
    gates_ref[rows, :] = _sigmoid(g).astype(gates_ref.dtype)


def _attn_dilated(hn, sub, w1_ref, b1_ref, w2_ref, b2_ref, gq_ref, gk_ref, cos1_ref, sin1_ref,
                  cos2_ref, sin2_ref, qkv1_ref, qkv2_ref, hs_ref, hs2_ref):
    tm = hn.shape[0]
    n_slabs = D_MODEL // LANES
    for j in range(n_slabs):
        hs_ref[sub, j] = hn[:, j * LANES:(j + 1) * LANES]

    def gather(src, start, count, stride):
        return jnp.concatenate(
            [src[sub, j, pl.ds(start, count, stride=stride), :] for j in range(n_slabs)], axis=1)

    coarse = None
    for (dil, w_ref, b_ref, gi, cos_ref, sin_ref, out_ref) in (
            (ATTN_PATTERNS[1][1], w1_ref, b1_ref, 1, cos1_ref, sin1_ref, qkv1_ref),
            (ATTN_PATTERNS[2][1], w2_ref, b2_ref, 2, cos2_ref, sin2_ref, qkv2_ref)):
        per = tm // dil
        if coarse is not None and dil % UNPERM_STEP == 0:
            q = tm // UNPERM_STEP
            for c_lo in range(UNPERM_STEP):
                for j in range(n_slabs):
                    hs2_ref[sub, j, c_lo * q:(c_lo + 1) * q, :] = (
                        coarse[c_lo][:, j * LANES:(j + 1) * LANES])
            classes = [gather(hs2_ref, (c % UNPERM_STEP) * q + c // UNPERM_STEP, per,
                              dil // UNPERM_STEP) for c in range(dil)]
        else:
            classes = [gather(hs_ref, c, per, dil) for c in range(dil)]
            if dil == UNPERM_STEP:
                coarse = classes
        hp = jnp.concatenate(classes, axis=0).astype(MM_DTYPE)
        acc = _dot(hp, w_ref[...]) + b_ref[...]
        part = pl.ds(sub * per, per)
        cos2 = cos_ref[:, part, :].reshape(tm, HEAD_DIM)
        sin2 = sin_ref[:, part, :].reshape(tm, HEAD_DIM)
        res = _qk_epilogue(acc, gq_ref[gi:gi + 1], gk_ref[gi:gi + 1], cos2, sin2)
        out_ref[:, part, :] = res.astype(out_ref.dtype).reshape(dil, per, GROUP_W)


def _in_proj_attn_kernel(x_ref,
                         wq0_ref, bq0_ref, wz_ref, bz_ref, wgt_ref, bgt_ref, w1_ref, b1_ref,
                         w2_ref, b2_ref, gq_ref, gk_ref,
                         cos0_ref, sin0_ref, cos1_ref, sin1_ref, cos2_ref, sin2_ref,
                         qkv0_ref, qkv1_ref, qkv2_ref, sz_ref, gates_ref, hs_ref, hs2_ref):
    ts = x_ref.shape[0] // IN_SUB
    for sub in range(IN_SUB):
        rows = pl.ds(sub * ts, ts)
        hn = _rms_normalize(x_ref[rows, :])
        _attn_natural(hn.astype(MM_DTYPE), rows, wq0_ref, bq0_ref, wz_ref, bz_ref, wgt_ref,
                      bgt_ref, gq_ref, gk_ref, cos0_ref, sin0_ref, qkv0_ref, sz_ref, gates_ref)
        _attn_dilated(hn, sub, w1_ref, b1_ref, w2_ref, b2_ref, gq_ref, gk_ref, cos1_ref,
                      sin1_ref, cos2_ref, sin2_ref, qkv1_ref, qkv2_ref, hs_ref, hs2_ref)


def _in_proj_attn(x, attn_w, tabs):
    B, S, _ = x.shape
    tm = TM_IN
    d1, d2 = ATTN_PATTERNS[1][1], ATTN_PATTERNS[2][1]
    (cos0, sin0), (cos1, sin1), (cos2, sin2) = tabs
    row = lambda w: pl.BlockSpec((None, tm, w), lambda b, i: (b, i, 0))
    tab_specs = [
        pl.BlockSpec((tm, HEAD_DIM), lambda b, i: (i, 0)),
        pl.BlockSpec((tm, HEAD_DIM), lambda b, i: (i, 0)),
        pl.BlockSpec((d1, tm // d1, HEAD_DIM), lambda b, i: (0, i, 0)),
        pl.BlockSpec((d1, tm // d1, HEAD_DIM), lambda b, i: (0, i, 0)),
        pl.BlockSpec((d2, tm // d2, HEAD_DIM), lambda b, i: (0, i, 0)),
        pl.BlockSpec((d2, tm // d2, HEAD_DIM), lambda b, i: (0, i, 0)),
    ]
    in_specs = [row(D_MODEL)] + [_const_spec(a.shape) for a in attn_w] + tab_specs
    out_shape = (
        jax.ShapeDtypeStruct((B, S, GROUP_W), MM_DTYPE),
        jax.ShapeDtypeStruct((B, d1, S // d1, GROUP_W), MM_DTYPE),
        jax.ShapeDtypeStruct((B, d2, S // d2, GROUP_W), MM_DTYPE),
        jax.ShapeDtypeStruct((B, S, ATT_W), MM_DTYPE),
        jax.ShapeDtypeStruct((B, S, 2 * D_MODEL), MM_DTYPE),
    )
    out_specs = (
        row(GROUP_W),
        pl.BlockSpec((None, d1, tm // d1, GROUP_W), lambda b, i: (b, 0, i, 0)),
        pl.BlockSpec((None, d2, tm // d2, GROUP_W), lambda b, i: (b, 0, i, 0)),
        row(ATT_W),
        row(2 * D_MODEL),
    )
    return pl.pallas_call(
        _in_proj_attn_kernel,
        grid=(B, S // tm),
        in_specs=in_specs,
        out_specs=out_specs,
        out_shape=out_shape,
        scratch_shapes=[pltpu.VMEM((IN_SUB, D_MODEL // LANES, tm // IN_SUB, LANES),
                                   jnp.float32)] * 2,
        compiler_params=_params(2),
        name="in_proj_attn",
    )(x, *attn_w, cos0, sin0, cos1, sin1, cos2, sin2)


def _rnn_kernel(x_ref, perm_ref, wx_ref, bx_ref, wz_ref, bz_ref, cw_ref, cb_ref, wg_ref, bg_ref,
                lam_ref, wo_ref, yr_ref, xs_ref, h_ref):
    nb, tt, _ = x_ref.shape
    ts = RNN_SUB_T
    n_sub = tt // ts
    rows = nb * ts
    hist = (CONV_W - 1) * nb
    step = pl.program_id(0)

    @pl.when(step == 0)
    def _():
        xs_ref[0:hist, :] = jnp.zeros((hist, D_RNN), jnp.float32)
        h_ref[...] = jnp.zeros_like(h_ref)

    neg_lam = -lam_ref[...]
    softplus = jnp.maximum(neg_lam, 0.0) + jnp.log1p(jnp.exp(-jnp.abs(neg_lam)))
    log2_a_scale = (-LRU_C * LOG2_E) * softplus

    def front(s):
        hn = jnp.concatenate(
            [_rms_normalize(x_ref[b, s * ts:(s + 1) * ts, :]) for b in range(nb)], axis=0)
        hp = _dot(perm_ref[0], hn.astype(MM_DTYPE)).astype(MM_DTYPE)
        xr = _dot(hp, wx_ref[...]) + bx_ref[...]
        z = _dot(hp, wz_ref[...]) + bz_ref[...]
        xs_ref[pl.ds(hist + s * rows, rows), :] = xr
        xc = cb_ref[...]
        for k in range(CONV_W):
            xc = xc + cw_ref[k:k + 1, :] * xs_ref[pl.ds(s * rows + k * nb, rows), :]
        xcb = xc.astype(MM_DTYPE)
        pre = [_dot(xcb[:, n * RNN_BLOCK_W:(n + 1) * RNN_BLOCK_W], wg_ref[n])
               for n in range(RNN_BLOCKS)]
        return xc, pre, z

    def back(s, staged, h):
        xc, pre, z = staged
        r = _sigmoid(jnp.concatenate([t[:, :RNN_BLOCK_W] for t in pre], axis=1)
                     + bg_ref[0:1, :])
        gi = _sigmoid(jnp.concatenate([t[:, RNN_BLOCK_W:] for t in pre], axis=1)
                      + bg_ref[1:2, :])
        t = log2_a_scale * r
        a = jnp.exp2(t)
        y1 = jnp.where(t > -EXPM1_SERIES_BELOW,
                       t * (EXPM1_C1 + EXPM1_C2 * t), 1.0 - a * a)
        mult = jnp.where(y1 > 0.0, y1 * lax.rsqrt(y1), 0.0)
        gx = gi * xc
        hs = []
        for t in range(ts):
            sl = slice(t * nb, (t + 1) * nb)
            m_t = mult[sl]
            if s == 0 and t == 0:
                m_t = jnp.where(step == 0, 1.0, m_t)
            h = a[sl] * h + m_t * gx[sl]
            hs.append(h)
        y = (jnp.concatenate(hs, axis=0) * (z * _sigmoid(z))).astype(MM_DTYPE)
        yr = _dot(y, wo_ref[...])
        yn = _dot(perm_ref[1], yr.astype(yr_ref.dtype)).astype(yr_ref.dtype)
        for b in range(nb):
            yr_ref[b, s * ts:(s + 1) * ts, :] = yn[b * ts:(b + 1) * ts]
        return h

    h = h_ref[...]
    staged = {0: front(0)}
    for s in range(n_sub):
        if s + 1 < n_sub:
            staged[s + 1] = front(s + 1)
        h = back(s, staged.pop(s), h)
    h_ref[...] = h
    xs_ref[0:hist, :] = xs_ref[pl.ds(n_sub * rows, hist), :]


def _time_major_perms(nb, ts):
    r = np.arange(nb * ts)
    p = np.zeros((nb * ts, nb * ts), np.float32)
    p[r, (r % nb) * ts + r // nb] = 1.0
    return jnp.asarray(np.stack([p, p.T]), MM_DTYPE)


def _rnn_branch(x, rnn_w):
    B, S, _ = x.shape
    tt = TT_RNN
    perms = _time_major_perms(B, RNN_SUB_T)
    return pl.pallas_call(
        _rnn_kernel,
        grid=(S // tt,),
        in_specs=([pl.BlockSpec((B, tt, D_MODEL), lambda i: (0, i, 0)),
                   _const_spec(perms.shape)]
                  + [_const_spec(a.shape) for a in rnn_w]),
        out_specs=pl.BlockSpec((B, tt, D_MODEL), lambda i: (0, i, 0)),
        out_shape=jax.ShapeDtypeStruct((B, S, D_MODEL), MM_DTYPE),
        scratch_shapes=[
            pltpu.VMEM(((CONV_W - 1) * B + B * tt, D_RNN), jnp.float32),
            pltpu.VMEM((B, D_RNN), jnp.float32),
        ],
        compiler_params=_params(1),
        name="rnn_branch",
    )(x, perms, *rnn_w)


def _attn_head(hd, q0, k0, v0, q1, k1, v1, q2, k2, v2, o_nat, l_nat, tmp_o, tmp_l):
    S = q0.shape[0]
    n_blk = S // ATT_BLK
    blk3 = (n_blk, ATT_BLK, HEAD_DIM)
    qi = lax.broadcasted_iota(jnp.int32, (1, ATT_BLK, ATT_BLK), 1)
    kj = lax.broadcasted_iota(jnp.int32, (1, ATT_BLK, ATT_BLK), 2)
    cur_ok = kj <= qi
    blk_id = lax.broadcasted_iota(jnp.int32, (n_blk, 1, 1), 0)
    qk_dims = (((2,), (2,)), ((0,), (0,)))
    pv_dims = (((2,), (1,)), ((0,), (0,)))

    for g, (q_ref, k_ref, v_ref) in enumerate(((q0, k0, v0), (q1, k1, v1), (q2, k2, v2))):
        dil = ATTN_PATTERNS[g][1]
        blocks_per_class = n_blk // dil
        q = q_ref[:, hd].reshape(blk3)
        k = k_ref[:, hd].reshape(blk3)
        v = v_ref[:, hd].reshape(blk3)
        v1 = jnp.concatenate([v, jnp.ones_like(v)], axis=2)
        if blocks_per_class > 1:
            shift = lambda t: jnp.concatenate([t[:1], t[:-1]], axis=0)
            kk = jnp.concatenate([shift(k), k], axis=1)
            vv = jnp.concatenate([shift(v1), v1], axis=1)
            s = lax.dot_general(q, kk, qk_dims, preferred_element_type=jnp.float32)
            first = (blk_id & (blocks_per_class - 1)) == 0
            prev_ok = kj >= qi + jnp.where(first, ATT_BLK, 0)
            s_prev = jnp.where(prev_ok, s[:, :, :ATT_BLK], NEG_INF)
            s_cur = jnp.where(cur_ok, s[:, :, ATT_BLK:], NEG_INF)
            m = jnp.max(jnp.maximum(s_prev, s_cur), axis=-1, keepdims=True)
            e = jnp.concatenate([jnp.exp2(s_prev - m), jnp.exp2(s_cur - m)], axis=2)
        else:
            vv = v1
            s = lax.dot_general(q, k, qk_dims, preferred_element_type=jnp.float32)
            s = jnp.where(cur_ok, s, NEG_INF)
            m = jnp.max(s, axis=-1, keepdims=True)
            e = jnp.exp2(s - m)
        od = lax.dot_general(e.astype(MM_DTYPE), vv, pv_dims,
                             preferred_element_type=jnp.float32)
        den = od[:, :, HEAD_DIM:]
        o = od[:, :, :HEAD_DIM] * (1.0 / den)
        lse = m + jnp.log2(den)
        if dil == 1:
            o_nat[g] = o.reshape(S, HEAD_DIM)
            l_nat[g] = lse.reshape(S, HEAD_DIM)
        elif blocks_per_class > 1 or dil % UNPERM_STEP:
            for nb in range(n_blk):
                c, m0 = divmod(nb, blocks_per_class)
                idx = pl.ds(m0 * ATT_BLK * dil + c, ATT_BLK, stride=dil)
                o_nat[g, idx, :] = o[nb]
                l_nat[g, idx, :] = lse[nb]
        else:
            for nb in range(n_blk):
                c_hi, c_lo = divmod(nb, UNPERM_STEP)
                idx = pl.ds(c_hi, ATT_BLK, stride=dil // UNPERM_STEP)
                tmp_o[c_lo, idx, :] = o[nb]
                tmp_l[c_lo, idx, :] = lse[nb]
            for c_lo in range(UNPERM_STEP):
                idx = pl.ds(c_lo, S // UNPERM_STEP, stride=UNPERM_STEP)
                o_nat[g, idx, :] = tmp_o[c_lo]
                l_nat[g, idx, :] = tmp_l[c_lo]


def _attn_kernel(q0, k0, v0, q1, k1, v1, q2, k2, v2, sz_ref, y_ref, o_nat, l_nat, tmp_o,
                 tmp_l):
    S = sz_ref.shape[0]
    for i in range(ATT_HEADS_PER_STEP):
        hd = slice(i * HEAD_DIM, (i + 1) * HEAD_DIM)
        _attn_head(hd, q0, k0, v0, q1, k1, v1, q2, k2, v2, o_nat.at[i], l_nat.at[i],
                   tmp_o.at[i], tmp_l.at[i])

    def merge(i, carry):
        rows = pl.ds(pl.multiple_of(i * COMBINE_ROWS, COMBINE_ROWS), COMBINE_ROWS)
        for j in range(ATT_HEADS_PER_STEP):
            hd = slice(j * HEAD_DIM, (j + 1) * HEAD_DIM)
            ls = [l_nat[j, g, rows, :] for g in range(N_GROUPS)]
            m = functools.reduce(jnp.maximum, ls)
            ws = [jnp.exp2(l - m) for l in ls]
            den = functools.reduce(lambda p, q: p + q, ws)
            att = functools.reduce(
                lambda p, q: p + q, [w * o_nat[j, g, rows, :] for g, w in enumerate(ws)])
            att = att * (1.0 / den)
            y_ref[rows, hd] = (att * sz_ref[rows, hd].astype(jnp.float32)).astype(y_ref.dtype)
        return carry
    lax.fori_loop(0, S // COMBINE_ROWS, merge, 0)


def _attention(qkv0, qkv1, qkv2, sz):
    B, S, _ = qkv0.shape
    hps = ATT_HEADS_PER_STEP
    w = hps * HEAD_DIM
    qkv1 = qkv1.reshape(B, S, GROUP_W)
    qkv2 = qkv2.reshape(B, S, GROUP_W)
    in_specs = []
    for _ in range(N_GROUPS):
        for part in range(3):
            in_specs.append(pl.BlockSpec(
                (None, S, w), lambda b, h, part=part: (b, 0, part * (HEADS // hps) + h)))
    in_specs.append(pl.BlockSpec((None, S, w), lambda b, h: (b, 0, h)))
    return pl.pallas_call(
        _attn_kernel,
        grid=(B, HEADS // hps),
        in_specs=in_specs,
        out_specs=pl.BlockSpec((None, S, w), lambda b, h: (b, 0, h)),
        out_shape=jax.ShapeDtypeStruct((B, S, ATT_W), MM_DTYPE),
        scratch_shapes=[
            pltpu.VMEM((hps, N_GROUPS, S, HEAD_DIM), jnp.float32),
            pltpu.VMEM((hps, N_GROUPS, S, HEAD_DIM), jnp.float32),
            pltpu.VMEM((hps, UNPERM_STEP, S // UNPERM_STEP, HEAD_DIM), jnp.float32),
            pltpu.VMEM((hps, UNPERM_STEP, S // UNPERM_STEP, HEAD_DIM), jnp.float32),
        ],
        compiler_params=_params(2),
        name="attention",
    )(qkv0, qkv0, qkv0, qkv1, qkv1, qkv1, qkv2, qkv2, qkv2, sz)


def _out_kernel(x_ref, p_ref, ya_ref, yr_ref, g_ref, woa_ref, wout_ref, wpg_ref, bpg_ref,
                wple_ref, o_ref):
    tm = x_ref.shape[0]
    halves = [pl.ds(i * (tm // OUT_SUB), tm // OUT_SUB) for i in range(OUT_SUB)]
    staged = []
    for rows in halves:
        pe = _dot(p_ref[rows, :].astype(MM_DTYPE), wple_ref[...])
        ya = _dot(ya_ref[rows, :], woa_ref[...])
        g0 = g_ref[rows, :D_MODEL].astype(jnp.float32)
        g1 = g_ref[rows, D_MODEL:].astype(jnp.float32)
        merged = g0 * yr_ref[rows, :].astype(jnp.float32) + g1 * ya
        x2 = x_ref[rows, :] + _dot(merged.astype(MM_DTYPE), wout_ref[...])
        staged.append((x2, pe))
    for rows, (x2, pe) in zip(halves, staged):
        n2 = _rms_normalize(x2).astype(MM_DTYPE)
        pg = _sigmoid(_dot(n2, wpg_ref[...]) + bpg_ref[...])
        o_ref[rows, :] = x2 + pg * pe


def _out_proj(x, p, ya, yr, gates, woa, wout, wpg, bpg, wple):
    B, S, _ = x.shape
    tm = TM_OUT
    row = lambda w: pl.BlockSpec((None, tm, w), lambda b, i: (b, i, 0))
    return pl.pallas_call(
        _out_kernel,
        grid=(B, S // tm),
        in_specs=[
            row(D_MODEL), row(PLE_DIM), row(ATT_W), row(D_MODEL), row(2 * D_MODEL),
            _const_spec(woa.shape), _const_spec(wout.shape), _const_spec(wpg.shape),
            _const_spec(bpg.shape), _const_spec(wple.shape),
        ],
        out_specs=row(D_MODEL),
        out_shape=jax.ShapeDtypeStruct((B, S, D_MODEL), x.dtype),
        compiler_params=_params(2),
        name="out_proj",
    )(x, p, ya, yr, gates, woa, wout, wpg, bpg, wple)


def _rope_tables(s):
    ang = (np.arange(s, dtype=np.float64)[:, None]
           * ROPE_THETA ** (-np.arange(0, HEAD_DIM, 2, dtype=np.float64) / HEAD_DIM)[None, :])
    cos, sin = np.cos(ang), np.sin(ang)
    cos2 = np.concatenate([cos, cos], axis=1).astype(np.float32)
    sin2 = np.concatenate([-sin, sin], axis=1).astype(np.float32)
    tabs = []
    for _, dil in ATTN_PATTERNS:
        if dil == 1:
            tabs.append((jnp.asarray(cos2), jnp.asarray(sin2)))
        else:
            perm = lambda t: jnp.asarray(
                np.ascontiguousarray(t.reshape(s // dil, dil, HEAD_DIM).transpose(1, 0, 2)))
            tabs.append((perm(cos2), perm(sin2)))
    return tabs


def kernel(x, p, norm_mix, w_in, b_in, conv_w, conv_b, w_rg_a, b_rg_a, w_rg_x, b_rg_x,
           lru_lambda, q_norm, k_norm, w_o_rnn, w_o_att, w_out, norm_ple, w_ple_gate,
           b_ple_gate, w_ple):
    depth = w_in.shape[0]
    s = x.shape[1]
    tabs = _rope_tables(s)
    f32 = jnp.float32
    for layer in range(depth):
        w = w_in[layer] * norm_mix[layer].astype(f32)[:, None]
        b = b_in[layer].astype(f32)[None, :]
        cols = lambda a, lo, hi: a[:, lo:hi]
        grp = lambda a, g: cols(a, OFF_QKV + g * GROUP_W, OFF_QKV + (g + 1) * GROUP_W)
        wb = lambda lo, hi: (cols(w, lo, hi).astype(MM_DTYPE), cols(b, lo, hi))
        gq = q_norm[layer].astype(f32) * (HEAD_DIM ** -0.5 * LOG2_E)
        gk = k_norm[layer].astype(f32)
        attn_w = (grp(w, 0).astype(MM_DTYPE), grp(b, 0),
                  *wb(OFF_Z_ATT, OFF_GATES), *wb(OFF_GATES, w.shape[1]),
                  grp(w, 1).astype(MM_DTYPE), grp(b, 1),
                  grp(w, 2).astype(MM_DTYPE), grp(b, 2), gq, gk)
        wg = jnp.concatenate([w_rg_a[layer], w_rg_x[layer]], axis=2).astype(MM_DTYPE)
        bg = jnp.stack([b_rg_a[layer], b_rg_x[layer]], axis=0).astype(f32)
        rnn_w = (*wb(0, OFF_Z_RNN), *wb(OFF_Z_RNN, OFF_QKV), conv_w[layer].astype(f32),
                 conv_b[layer].astype(f32)[None, :], wg, bg,
                 lru_lambda[layer].astype(f32)[None, :], w_o_rnn[layer].astype(MM_DTYPE))
        qkv0, qkv1, qkv2, sz, gates = _in_proj_attn(x, attn_w, tabs)
        yr = _rnn_branch(x, rnn_w)
        ya = _attention(qkv0, qkv1, qkv2, sz)
        x = _out_proj(
            x, p[layer], ya, yr, gates,
            w_o_att[layer].astype(MM_DTYPE), w_out[layer].astype(MM_DTYPE),
            (w_ple_gate[layer] * norm_ple[layer].astype(f32)[:, None]).astype(MM_DTYPE),
            b_ple_gate[layer].astype(f32)[None, :], w_ple[layer].astype(MM_DTYPE))
    return x
```
